```python
import math
import jax
import jax.numpy as jnp
from jax import lax
import numpy as np

D_MODEL = 1024
BATCH = 8
SEQ = 2048
DEPTH = 2
DEC_BATCH = 32
DEC_SEQ = 8
PAST_LEN = 8192
PAGE_SIZE = 128

N_META = 16
D_FF = 2816
SSD_HEADS = 8
SSD_HEAD_DIM = 64
SSD_INNER = SSD_HEADS * SSD_HEAD_DIM
SSD_GROUPS = 2
SSD_STATE = 64
SSD_CONV = 4
SSD_CONV_DIM = SSD_INNER + 2 * SSD_GROUPS * SSD_STATE
CHUNK = 128
SB_HEADS = 8
SB_HEAD_DIM = 64
SB_WIDTH = SB_HEADS * SB_HEAD_DIM
BLOCK_Q = 128
D_MIX = SSD_INNER + SB_WIDTH
IN_COLS = SSD_INNER + SSD_CONV_DIM + SSD_HEADS + 3 * SB_WIDTH
EPS = 1e-6

kernel_name = 'hymba_ssd_stickbreaking_macaron_step'


def rms_norm(x, g):
    xf = x.astype(jnp.float32)
    y = xf * lax.rsqrt(jnp.mean(xf * xf, axis=-1, keepdims=True) + EPS)
    return (y * g.astype(jnp.float32)).astype(x.dtype)


def swiglu(h, w_gu, w_down):
    g, u = jnp.split(h @ w_gu, 2, axis=-1)
    return (jax.nn.silu(g) * u) @ w_down


def causal_conv(u, buf, w, b):
    L = u.shape[1]
    up = jnp.concatenate([buf.astype(u.dtype), u], axis=1)
    acc = up[:, 0:L] * w[0]
    for j in range(1, SSD_CONV):
        acc = acc + up[:, j:j + L] * w[j]
    return jax.nn.silu(acc + b), up[:, L:]


def ssd_scan(x, dt, A, Bm, Cm, h0):
    b, L, nh, p = x.shape
    n = Bm.shape[-1]
    Q = CHUNK if L % CHUNK == 0 else L
    c = L // Q
    x = x.reshape(b, c, Q, nh, p)
    dt = dt.reshape(b, c, Q, nh)
    Bm = Bm.reshape(b, c, Q, nh, n)
    Cm = Cm.reshape(b, c, Q, nh, n)
    cs = jnp.cumsum(dt * A, axis=2)
    xdt = x * dt[..., None]
    seg = cs[:, :, :, None, :] - cs[:, :, None, :, :]
    causal = jnp.tril(jnp.ones((Q, Q), dtype=bool))[None, None, :, :, None]
    decay = jnp.exp(jnp.where(causal, seg, -jnp.inf))
    scores = jnp.einsum('bcthn,bcshn->bctsh', Cm, Bm) * decay
    y_diag = jnp.einsum('bctsh,bcshp->bcthp', scores, xdt)
    to_end = jnp.exp(cs[:, :, -1:, :] - cs)
    states = jnp.einsum('bcshn,bcsh,bcshp->bchpn', Bm, to_end, xdt)
    chunk_decay = jnp.exp(cs[:, :, -1, :])

    def step(h_prev, inp):
        st, dec = inp
        return h_prev * dec[..., None, None] + st, h_prev

    h_last, h_prev = lax.scan(step, h0, (jnp.moveaxis(states, 1, 0), jnp.moveaxis(chunk_decay, 1, 0)))
    h_prev = jnp.moveaxis(h_prev, 0, 1)
    y_off = jnp.einsum('bcthn,bcth,bchpn->bcthp', Cm, jnp.exp(cs), h_prev)
    return (y_diag + y_off).reshape(b, L, nh, p), h_last


def sb_block(q, start, k, v, key_valid, bias):
    qlen, klen = q.shape[1], k.shape[1]
    z = jnp.einsum('bqhd,bkhd->bhqk', q, k).astype(jnp.float32) * (SB_HEAD_DIM ** -0.5)
    z = z + bias.astype(jnp.float32)[None, :, None, None]
    qpos = start + jnp.arange(qlen)
    kpos = jnp.arange(klen)
    allowed = (kpos[None, :] < qpos[:, None]) & key_valid[None, :]
    log_stay = jnp.where(allowed, jax.nn.log_sigmoid(-z), 0.0)
    log_after = lax.cumsum(log_stay, axis=3, reverse=True) - log_stay
    w = jnp.where(allowed, jnp.exp(jax.nn.log_sigmoid(z) + log_after), 0.0)
    return jnp.einsum('bhqk,bkhd->bqhd', w.astype(v.dtype), v)


def mixer(h, lp, conv_buf, ssm_h0, k_past, v_past, front_pad):
    b, L, _ = h.shape
    f32 = jnp.float32
    sizes = (SSD_INNER, SSD_CONV_DIM, SSD_HEADS, SB_WIDTH, SB_WIDTH, SB_WIDTH)
    z, xbc, dt_raw, q, k, v = jnp.split(h @ lp['w_in'], np.cumsum(sizes)[:-1].tolist(), axis=-1)
    xbc_c, new_conv = causal_conv(xbc, conv_buf, lp['conv_w'], lp['conv_b'])
    xs, Bm, Cm = jnp.split(xbc_c, [SSD_INNER, SSD_INNER + SSD_GROUPS * SSD_STATE], axis=-1)
    hpg = SSD_HEADS // SSD_GROUPS
    xs = xs.reshape(b, L, SSD_HEADS, SSD_HEAD_DIM).astype(f32)
    Bm = jnp.repeat(Bm.reshape(b, L, SSD_GROUPS, SSD_STATE), hpg, axis=2).astype(f32)
    Cm = jnp.repeat(Cm.reshape(b, L, SSD_GROUPS, SSD_STATE), hpg, axis=2).astype(f32)
    dt = jax.nn.softplus(dt_raw.astype(f32) + lp['dt_bias'].astype(f32))
    A = -jnp.exp(lp['A_log'].astype(f32))

    def padf(a):
        return jnp.pad(a, [(0, 0), (front_pad, 0)] + [(0, 0)] * (a.ndim - 2))

    y, h_last = ssd_scan(padf(xs), padf(dt), A, padf(Bm), padf(Cm), ssm_h0.astype(f32))
    y = y[:, front_pad:] + lp['D_skip'].astype(f32)[:, None] * xs
    y = y.reshape(b, L, SSD_INNER) * jax.nn.silu(z.astype(f32))
    yg = y.reshape(b, L, SSD_GROUPS, SSD_INNER // SSD_GROUPS)
    yg = yg * lax.rsqrt(jnp.mean(yg * yg, axis=-1, keepdims=True) + EPS)
    y_ssd = (yg.reshape(b, L, SSD_INNER) * lp['ssd_norm'].astype(f32)).astype(h.dtype)
    q = rms_norm(q.reshape(b, L, SB_HEADS, SB_HEAD_DIM), lp['q_norm'])
    k = rms_norm(k.reshape(b, L, SB_HEADS, SB_HEAD_DIM), lp['k_norm'])
    v = v.reshape(b, L, SB_HEADS, SB_HEAD_DIM)
    bias = lp['sb_bias']
    if k_past is None:
        P = front_pad + L
        pad4 = ((0, 0), (front_pad, 0), (0, 0), (0, 0))
        qp, kp, vp = jnp.pad(q, pad4), jnp.pad(k, pad4), jnp.pad(v, pad4)
        valid = jnp.arange(P) >= front_pad
        nb = P // BLOCK_Q
        q_blocks = jnp.moveaxis(qp.reshape(b, nb, BLOCK_Q, SB_HEADS, SB_HEAD_DIM), 1, 0)
        starts = jnp.arange(nb) * BLOCK_Q
        o = lax.map(lambda a: sb_block(a[0], a[1], kp, vp, valid, bias), (q_blocks, starts))
        o = jnp.moveaxis(o, 0, 1).reshape(b, P, SB_HEADS, SB_HEAD_DIM)[:, front_pad:]
    else:
        k_all = jnp.concatenate([k_past.astype(k.dtype), k], axis=1)
        v_all = jnp.concatenate([v_past.astype(v.dtype), v], axis=1)
        valid = jnp.ones((k_all.shape[1],), dtype=bool)
        o = sb_block(q, k_past.shape[1], k_all, v_all, valid, bias)
    y_sb = rms_norm(o.reshape(b, L, SB_WIDTH), lp['sb_out_norm'])
    out = jnp.concatenate([y_ssd, y_sb], axis=-1) @ lp['w_out']
    return out, new_conv, h_last, k, v


def layer(x, lp, conv_buf, ssm_h0, k_past, v_past, front_pad):
    x = x + 0.5 * swiglu(rms_norm(x, lp['norm_ffn1']), lp['ffn1_w_gu'], lp['ffn1_w_down'])
    m, new_conv, h_last, k, v = mixer(rms_norm(x, lp['norm_mix']), lp, conv_buf, ssm_h0, k_past, v_past, front_pad)
    x = x + m
    x = x + 0.5 * swiglu(rms_norm(x, lp['norm_ffn2']), lp['ffn2_w_gu'], lp['ffn2_w_down'])
    return x, new_conv, h_last, k, v


def setup_inputs(seed: int = 0) -> dict:
    key = jax.random.key(seed)
    ks = iter(jax.random.split(key, 32))
    f32 = jnp.float32
    n_pages = PAST_LEN // PAGE_SIZE
    n_phys = (5 * DEC_BATCH * n_pages + 3) // 4

    def nrm(shape, scale):
        return scale * jax.random.normal(next(ks), shape, f32)

    def gain(shape):
        return 1.0 + nrm(shape, 0.02)

    x_prompt = nrm((BATCH, SEQ, D_MODEL), 1.0)
    x_sample = nrm((DEC_BATCH, DEC_SEQ, D_MODEL), 1.0)
    cache_k = nrm((DEPTH, n_phys, PAGE_SIZE, SB_HEADS, SB_HEAD_DIM), 1.0)
    cache_v = nrm((DEPTH, n_phys, PAGE_SIZE, SB_HEADS, SB_HEAD_DIM), 1.0)
    state_ssm = nrm((DEPTH, DEC_BATCH, SSD_HEADS, SSD_HEAD_DIM, SSD_STATE), 0.1)
    state_conv = nrm((DEPTH, DEC_BATCH, SSD_CONV - 1, SSD_CONV_DIM), 1.0)
    page_table = jax.random.permutation(next(ks), n_phys)[:DEC_BATCH * n_pages].reshape(DEC_BATCH, n_pages).astype(jnp.int32)
    meta_tokens = nrm((N_META, D_MODEL), 1.0)
    norm_ffn1 = gain((DEPTH, D_MODEL))
    ffn1_w_gu = nrm((DEPTH, D_MODEL, 2 * D_FF), D_MODEL ** -0.5)
    ffn1_w_down = nrm((DEPTH, D_FF, D_MODEL), D_FF ** -0.5)
    norm_mix = gain((DEPTH, D_MODEL))
    w_in = nrm((DEPTH, D_MODEL, IN_COLS), D_MODEL ** -0.5)
    conv_w = nrm((DEPTH, SSD_CONV, SSD_CONV_DIM), SSD_CONV ** -0.5)
    conv_b = nrm((DEPTH, SSD_CONV_DIM), 0.02)
    dt0 = jnp.exp(jax.random.uniform(next(ks), (DEPTH, SSD_HEADS), f32, math.log(1e-3), math.log(1e-1)))
    dt_bias = dt0 + jnp.log(-jnp.expm1(-dt0))
    A_log = jnp.log(jax.random.uniform(next(ks), (DEPTH, SSD_HEADS), f32, 1.0, 16.0))
    D_skip = gain((DEPTH, SSD_HEADS))
    ssd_norm = gain((DEPTH, SSD_INNER))
    q_norm = gain((DEPTH, SB_HEAD_DIM))
    k_norm = gain((DEPTH, SB_HEAD_DIM))
    sb_bias = jax.random.uniform(next(ks), (DEPTH, SB_HEADS), f32, -7.0, -5.0)
    sb_out_norm = gain((DEPTH, SB_WIDTH))
    w_out = nrm((DEPTH, D_MIX, D_MODEL), (2 * DEPTH * D_MIX) ** -0.5)
    norm_ffn2 = gain((DEPTH, D_MODEL))
    ffn2_w_gu = nrm((DEPTH, D_MODEL, 2 * D_FF), D_MODEL ** -0.5)
    ffn2_w_down = nrm((DEPTH, D_FF, D_MODEL), D_FF ** -0.5)
    return {'x_prompt': x_prompt, 'x_sample': x_sample, 'cache_k': cache_k, 'cache_v': cache_v,
            'state_ssm': state_ssm, 'state_conv': state_conv, 'page_table': page_table,
            'meta_tokens': meta_tokens, 'norm_ffn1': norm_ffn1, 'ffn1_w_gu': ffn1_w_gu,
            'ffn1_w_down': ffn1_w_down, 'norm_mix': norm_mix, 'w_in': w_in, 'conv_w': conv_w,
            'conv_b': conv_b, 'dt_bias': dt_bias, 'A_log': A_log, 'D_skip': D_skip,
            'ssd_norm': ssd_norm, 'q_norm': q_norm, 'k_norm': k_norm, 'sb_bias': sb_bias,
            'sb_out_norm': sb_out_norm, 'w_out': w_out, 'norm_ffn2': norm_ffn2,
            'ffn2_w_gu': ffn2_w_gu, 'ffn2_w_down': ffn2_w_down}


def reference(x_prompt, x_sample, cache_k, cache_v, state_ssm, state_conv, page_table, meta_tokens,
              norm_ffn1, ffn1_w_gu, ffn1_w_down, norm_mix, w_in, conv_w, conv_b, dt_bias, A_log,
              D_skip, ssd_norm, q_norm, k_norm, sb_bias, sb_out_norm, w_out, norm_ffn2, ffn2_w_gu,
              ffn2_w_down):
    params = {'norm_ffn1': norm_ffn1, 'ffn1_w_gu': ffn1_w_gu, 'ffn1_w_down': ffn1_w_down,
              'norm_mix': norm_mix, 'w_in': w_in, 'conv_w': conv_w, 'conv_b': conv_b,
              'dt_bias': dt_bias, 'A_log': A_log, 'D_skip': D_skip, 'ssd_norm': ssd_norm,
              'q_norm': q_norm, 'k_norm': k_norm, 'sb_bias': sb_bias, 'sb_out_norm': sb_out_norm,
              'w_out': w_out, 'norm_ffn2': norm_ffn2, 'ffn2_w_gu': ffn2_w_gu,
              'ffn2_w_down': ffn2_w_down}
    bp = x_prompt.shape[0]
    db = x_sample.shape[0]
    meta = jnp.broadcast_to(meta_tokens[None].astype(x_prompt.dtype), (bp, N_META, D_MODEL))
    xp = jnp.concatenate([meta, x_prompt], axis=1)
    front_pad = (-xp.shape[1]) % CHUNK
    xs = x_sample
    kp_l, vp_l, sp_l, cp_l, ks_l, vs_l, ss_l, cs_l = [], [], [], [], [], [], [], []
    for l in range(DEPTH):
        lp = {name: arr[l] for name, arr in params.items()}
        conv0 = jnp.zeros((bp, SSD_CONV - 1, SSD_CONV_DIM), xp.dtype)
        ssm0 = jnp.zeros((bp, SSD_HEADS, SSD_HEAD_DIM, SSD_STATE), jnp.float32)
        xp, c_p, s_p, k_p, v_p = layer(xp, lp, conv0, ssm0, None, None, front_pad)
        k_past = cache_k[l][page_table].reshape(db, -1, SB_HEADS, SB_HEAD_DIM)
        v_past = cache_v[l][page_table].reshape(db, -1, SB_HEADS, SB_HEAD_DIM)
        xs, c_s, s_s, k_s, v_s = layer(xs, lp, state_conv[l], state_ssm[l], k_past, v_past, 0)
        kp_l.append(k_p); vp_l.append(v_p); sp_l.append(s_p); cp_l.append(c_p)
        ks_l.append(k_s); vs_l.append(v_s); ss_l.append(s_s); cs_l.append(c_s)
    y_prompt = xp[:, N_META:]
    return (y_prompt, xs, jnp.stack(kp_l), jnp.stack(vp_l), jnp.stack(sp_l), jnp.stack(cp_l),
            jnp.stack(ks_l), jnp.stack(vs_l), jnp.stack(ss_l), jnp.stack(cs_l))
```

```python
import functools

import jax
import jax.numpy as jnp
from jax import lax
from jax.experimental import pallas as pl
from jax.experimental.pallas import tpu as pltpu

F32 = jnp.float32
BF16 = jnp.bfloat16
EPS = 1e-6

LANES = 128
HEAD_DIM = 64
N_HEADS = 8
WIDTH = N_HEADS * HEAD_DIM
N_PAIRS = WIDTH // LANES
SSD_GROUPS = 2
CONV_W = 4
CONV_DIM = WIDTH + 2 * SSD_GROUPS * HEAD_DIM
DT_PAD = LANES
CHUNK = 128
VMEM_LIMIT = 56 * 1024 * 1024


def _dot(a, b):
    return jnp.dot(a, b, preferred_element_type=F32)


def _dot_nt(a, b):
    return lax.dot_general(a, b, (((1,), (1,)), ((), ())), preferred_element_type=F32)


def _dot_tn(a, b):
    return lax.dot_general(a, b, (((0,), (0,)), ((), ())), preferred_element_type=F32)


def _split_dot(a, m, passes):
    out = None
    rem = a
    for p in range(passes):
        part = rem.astype(BF16)
        term = _dot(part, m)
        out = term if out is None else out + term
        if p + 1 < passes:
            rem = rem - part.astype(F32)
    return out


def _rms(x, g):
    ms = jnp.mean(x * x, axis=-1, keepdims=True)
    return x * lax.rsqrt(ms + EPS) * g


def _silu(x):
    return x * jax.nn.sigmoid(x)


def _softplus(x):
    return jnp.maximum(x, 0.0) + jnp.log1p(jnp.exp(-jnp.abs(x)))


def _iota(shape, dim):
    return lax.broadcasted_iota(jnp.int32, shape, dim)


def _pick_tile(n, cap, mult):
    if n <= cap:
        return n
    best = None
    for t in range(mult, cap + 1, mult):
        if n % t == 0:
            best = t
    assert best is not None, (n, cap, mult)
    return best


def _resident(shape, index_map):
    return pl.BlockSpec(shape, index_map, pipeline_mode=pl.Buffered(1))


def _ffn_body(has_mix, n_chunks, fc, *refs):
    if has_mix:
        x_ref, ys_ref, yb_ref, wo_ref, g_ref, wgu_ref, wd_ref, o_ref = refs
        x = (x_ref[...]
             + _dot(ys_ref[...].astype(BF16), wo_ref[:WIDTH, :])
             + _dot(yb_ref[...].astype(BF16), wo_ref[WIDTH:, :]))
    else:
        x_ref, g_ref, wgu_ref, wd_ref, o_ref = refs
        x = x_ref[...]
    h = _rms(x, g_ref[...]).astype(BF16)
    d_ff = n_chunks * fc
    acc = None
    for c in range(n_chunks):
        gate = _dot(h, wgu_ref[:, c * fc:(c + 1) * fc])
        up = _dot(h, wgu_ref[:, d_ff + c * fc:d_ff + (c + 1) * fc])
        a = (_silu(gate) * up).astype(BF16)
        d = _dot(a, wd_ref[c * fc:(c + 1) * fc, :])
        acc = d if acc is None else acc + d
    o_ref[...] = x + 0.5 * acc


def _ffn(x, layer, norm, wgu, wd, mix=None):
    t, d = x.shape
    d_ff = wd.shape[1]
    fc = 256
    assert d_ff % fc == 0
    tm = _pick_tile(t, 704, 16)
    row = lambda i: (i, 0)
    lay3 = lambda i: (layer, 0, 0)
    in_specs = [pl.BlockSpec((tm, d), row)]
    args = [x]
    if mix is not None:
        ys, yb, wo = mix
        in_specs += [pl.BlockSpec((tm, WIDTH), row), pl.BlockSpec((tm, WIDTH), row),
                     _resident((None, 2 * WIDTH, d), lay3)]
        args += [ys, yb, wo]
    in_specs += [_resident((None, 1, d), lay3), _resident((None, d, 2 * d_ff), lay3),
                 _resident((None, d_ff, d), lay3)]
    args += [norm, wgu, wd]
    return pl.pallas_call(
        functools.partial(_ffn_body, mix is not None, d_ff // fc, fc),
        grid=(t // tm,),
        in_specs=in_specs,
        out_specs=pl.BlockSpec((tm, d), row),
        out_shape=jax.ShapeDtypeStruct((t, d), F32),
        compiler_params=pltpu.CompilerParams(dimension_semantics=("parallel",), vmem_limit_bytes=VMEM_LIMIT),
        name="ffn_mix" if mix is not None else "ffn",
    )(*args)


_C_Z, _C_XBC, _C_Q, _C_K, _C_V, _C_DT, _C_END = 0, 512, 1280, 1792, 2304, 2816, 2944


def _head_norm(t, gain, ones_bd):
    ms = _split_dot(t * t, ones_bd, 2) * (1.0 / HEAD_DIM)
    return t * lax.rsqrt(ms + EPS) * gain


def _mixin_body(seq, tl, x_ref, g_ref, w_ref, qn_ref, kn_ref, bd_ref,
                z_ref, xbc_ref, dt_ref, k_ref, v_ref, qa_ref, ka_ref, va_ref):
    j = pl.program_id(1)
    h = _rms(x_ref[0], g_ref[...]).astype(BF16)
    p = _dot(h, w_ref[...])
    z_ref[0] = p[:, _C_Z:_C_XBC]
    xbc_ref[0] = p[:, _C_XBC:_C_Q]
    dt_ref[0] = p[:, _C_DT:_C_END]
    bd = bd_ref[...]
    q = _head_norm(p[:, _C_Q:_C_K], qn_ref[...], bd)
    k = _head_norm(p[:, _C_K:_C_V], kn_ref[...], bd)
    v = p[:, _C_V:_C_DT]
    k_ref[0] = k
    v_ref[0] = v
    valid = (j * tl + _iota((tl, 1), 0)) < seq
    qa_ref[0] = jnp.where(valid, q * (HEAD_DIM ** -0.5), 0.0).astype(BF16)
    ka_ref[0] = jnp.where(valid, k, 0.0).astype(BF16)
    va_ref[0] = jnp.where(valid, v, 0.0).astype(BF16)


def _mixin(x, layer, norm, w_in, qn, kn, ones_bd, seq_pad):
    nb, seq, d = x.shape
    tl = _pick_tile(seq_pad, 640, 16)
    blk = lambda b, j: (b, j, 0)
    lay3 = lambda b, j: (layer, 0, 0)
    f32_out = lambda c: jax.ShapeDtypeStruct((nb, seq, c), F32)
    att_out = jax.ShapeDtypeStruct((nb, seq_pad, WIDTH), BF16)
    return pl.pallas_call(
        functools.partial(_mixin_body, seq, tl),
        grid=(nb, seq_pad // tl),
        in_specs=[pl.BlockSpec((1, tl, d), blk), _resident((None, 1, d), lay3),
                  _resident((None, d, _C_END), lay3), _resident((None, 1, WIDTH), lay3),
                  _resident((None, 1, WIDTH), lay3), _resident((WIDTH, WIDTH), lambda b, j: (0, 0))],
        out_specs=[pl.BlockSpec((1, tl, WIDTH), blk), pl.BlockSpec((1, tl, CONV_DIM), blk),
                   pl.BlockSpec((1, tl, DT_PAD), blk), pl.BlockSpec((1, tl, WIDTH), blk),
                   pl.BlockSpec((1, tl, WIDTH), blk), pl.BlockSpec((1, tl, WIDTH), blk),
                   pl.BlockSpec((1, tl, WIDTH), blk), pl.BlockSpec((1, tl, WIDTH), blk)],
        out_shape=[f32_out(WIDTH), f32_out(CONV_DIM), f32_out(DT_PAD), f32_out(WIDTH), f32_out(WIDTH),
                   att_out, att_out, att_out],
        compiler_params=pltpu.CompilerParams(dimension_semantics=("parallel", "parallel"),
                                             vmem_limit_bytes=VMEM_LIMIT),
        name="mix_in",
    )(x, norm, w_in, qn, kn, ones_bd)


def _ssd_body(seq, q, xbc_ref, z_ref, dt_ref, cinit_ref, sinit_ref, cw_ref, cb_ref, dtb_ref, alog_ref,
              dsk_ref, nw_ref, exp_ref, y_ref, sout_ref, cbuf, st):
    c = pl.program_id(1)

    @pl.when(c == 0)
    def _():
        cbuf[0:8, :] = cinit_ref[0]
        st[...] = sinit_ref[0]

    valid = (c * q + _iota((q, 1), 0)) < seq
    xraw = jnp.where(valid, xbc_ref[0], 0.0)
    cbuf[8:8 + q, :] = xraw
    cw = cw_ref[...]
    conv = (cbuf[5:5 + q, :] * cw[0:1] + cbuf[6:6 + q, :] * cw[1:2]
            + cbuf[7:7 + q, :] * cw[2:3] + xraw * cw[3:4])
    cbuf[5:8, :] = cbuf[5 + q:8 + q, :]
    xc = _silu(conv + cb_ref[...])
    xs = xc[:, :WIDTH]
    bm = xc[:, WIDTH:WIDTH + LANES].astype(BF16)
    cm = xc[:, WIDTH + LANES:].astype(BF16)

    dt = jnp.where(valid, _softplus(dt_ref[0] + dtb_ref[...]), 0.0)
    a = dt * (-jnp.exp(alog_ref[...]))
    tri = jnp.where(_iota((q, q), 0) >= _iota((q, q), 1), 1.0, 0.0).astype(BF16)
    cs = _split_dot_left(tri, a)
    ecs = jnp.exp(cs)
    to_end = jnp.exp(cs[q - 1:q, :] - cs)
    expand = exp_ref[...]
    dt_e = _split_dot(dt, expand, 2)
    ecs_e = _split_dot(ecs, expand, 2)
    to_end_e = _split_dot(to_end, expand, 2)
    xdt = xs * dt_e

    lane = _iota((q, LANES), 1)
    low = lane < HEAD_DIM
    g0 = _dot_nt(jnp.where(low, cm, jnp.zeros_like(cm)), bm)
    g1 = _dot_nt(jnp.where(low, jnp.zeros_like(cm), cm), bm)
    cs_t = cs.T
    causal = _iota((q, q), 0) >= _iota((q, q), 1)
    xdt_b = xdt.astype(BF16)
    y_pairs = []
    for p in range(N_PAIRS):
        xp = xdt_b[:, p * LANES:(p + 1) * LANES]
        yp = None
        for h, xm in ((2 * p, jnp.where(low, xp, jnp.zeros_like(xp))),
                      (2 * p + 1, jnp.where(low, jnp.zeros_like(xp), xp))):
            seg = cs[:, h:h + 1] - cs_t[h:h + 1, :]
            decay = jnp.exp(jnp.where(causal, seg, -jnp.inf))
            m = ((g0 if h < N_HEADS // SSD_GROUPS else g1) * decay).astype(BF16)
            t = _dot(m, xm)
            yp = t if yp is None else yp + t
        y_pairs.append(yp)
    y_diag = jnp.concatenate(y_pairs, axis=1)

    s_prev = st[...]
    y_off = _dot(cm, s_prev.astype(BF16)) * ecs_e
    y = y_diag + y_off + dsk_ref[...] * xs
    zz = jnp.where(valid, z_ref[0], 0.0)
    y = y * _silu(zz)
    half = WIDTH // SSD_GROUPS
    y0, y1 = y[:, :half], y[:, half:]
    y0 = y0 * lax.rsqrt(jnp.mean(y0 * y0, axis=-1, keepdims=True) + EPS)
    y1 = y1 * lax.rsqrt(jnp.mean(y1 * y1, axis=-1, keepdims=True) + EPS)
    y_ref[0] = (jnp.concatenate([y0, y1], axis=1) * nw_ref[...]).astype(y_ref.dtype)

    upd = _dot_tn(bm, (xdt * to_end_e).astype(BF16))
    own = (_iota((LANES, WIDTH), 0) < HEAD_DIM) == (_iota((LANES, WIDTH), 1) < half)
    st[...] = s_prev * ecs_e[q - 1:q, :] + jnp.where(own, upd, 0.0)

    @pl.when(c == pl.num_programs(1) - 1)
    def _():
        sout_ref[0] = st[...]


def _split_dot_left(m, a):
    a1 = a.astype(BF16)
    r1 = a - a1.astype(F32)
    a2 = r1.astype(BF16)
    a3 = (r1 - a2.astype(F32)).astype(BF16)
    return _dot(m, a1) + _dot(m, a2) + _dot(m, a3)


def _ssd(xbc, z, dt, seq, conv_init, state_init, layer, cw, cb, dtb, alog, dsk, nw, expand):
    nb, n_rows, _ = xbc.shape
    q = CHUNK
    nchunk = pl.cdiv(n_rows, q)
    blk = lambda b, c: (b, c, 0)
    per_b = lambda b, c: (b, 0, 0)
    lay3 = lambda b, c: (layer, 0, 0)
    return pl.pallas_call(
        functools.partial(_ssd_body, seq, q),
        grid=(nb, nchunk),
        in_specs=[pl.BlockSpec((1, q, CONV_DIM), blk), pl.BlockSpec((1, q, WIDTH), blk),
                  pl.BlockSpec((1, q, DT_PAD), blk), pl.BlockSpec((1, 8, CONV_DIM), per_b),
                  pl.BlockSpec((1, LANES, WIDTH), per_b), pl.BlockSpec((None, 8, CONV_DIM), lay3),
                  pl.BlockSpec((None, 1, CONV_DIM), lay3), pl.BlockSpec((None, 1, DT_PAD), lay3),
                  pl.BlockSpec((None, 1, DT_PAD), lay3), pl.BlockSpec((None, 1, WIDTH), lay3),
                  pl.BlockSpec((None, 1, WIDTH), lay3), pl.BlockSpec((LANES, WIDTH), lambda b, c: (0, 0))],
        out_specs=[pl.BlockSpec((1, q, WIDTH), blk), pl.BlockSpec((1, LANES, WIDTH), per_b)],
        out_shape=[jax.ShapeDtypeStruct((nb, n_rows, WIDTH), BF16),
                   jax.ShapeDtypeStruct((nb, LANES, WIDTH), F32)],
        scratch_shapes=[pltpu.VMEM((8 + q, CONV_DIM), F32), pltpu.VMEM((LANES, WIDTH), F32)],
        compiler_params=pltpu.CompilerParams(dimension_semantics=("parallel", "arbitrary"),
                                             vmem_limit_bytes=VMEM_LIMIT),
        name="ssd",
    )(xbc, z, dt, conv_init, state_init, cw, cb, dtb, alog, dsk, nw, expand)


def _state_to_blocks(s):
    nb = s.shape[0]
    hpg = N_HEADS // SSD_GROUPS
    t = jnp.transpose(s.reshape(nb, SSD_GROUPS, hpg, HEAD_DIM, HEAD_DIM), (0, 1, 4, 2, 3))
    t = t.reshape(nb, SSD_GROUPS, HEAD_DIM, hpg * HEAD_DIM)
    zero = jnp.zeros_like(t[:, 0])
    top = jnp.concatenate([t[:, 0], zero], axis=-1)
    bot = jnp.concatenate([zero, t[:, 1]], axis=-1)
    return jnp.concatenate([top, bot], axis=1)


def _blocks_to_state(sb):
    nb = sb.shape[0]
    hpg = N_HEADS // SSD_GROUPS
    half = WIDTH // SSD_GROUPS
    t = jnp.stack([sb[:, :HEAD_DIM, :half], sb[:, HEAD_DIM:, half:]], axis=1)
    t = t.reshape(nb, SSD_GROUPS, HEAD_DIM, hpg, HEAD_DIM)
    return jnp.transpose(t, (0, 1, 3, 4, 2)).reshape(nb, N_HEADS, HEAD_DIM, HEAD_DIM)


def _sb_scores(z):
    return jnp.minimum(-z, 0.0) - jnp.log1p(jnp.exp(-jnp.abs(z)))


def _suffix_matrix(tk):
    j = _iota((tk, 2 * tk), 0)
    s = _iota((tk, 2 * tk), 1)
    return jnp.where((s >= tk) | (j >= s), 1.0, 0.0).astype(BF16)


def _attn_body(q_ref, k_ref, v_ref, bias_ref, g_ref, o_ref, r_sc, acc_sc):
    i = pl.program_id(1)
    tq = CHUNK
    rows = 2 * tq
    low = _iota((tq, LANES), 1) < HEAD_DIM
    row = _iota((rows, CHUNK), 0)
    col = _iota((rows, CHUNK), 1)
    top = row < tq
    strictly_before = col < jnp.where(top, row, row - tq)
    u2 = _suffix_matrix(CHUNK)

    def block(kb, diagonal):
        ks = pl.multiple_of(kb * CHUNK, CHUNK)
        for p in range(N_PAIRS):
            lanes = slice(p * LANES, (p + 1) * LANES)
            qp = q_ref[0, :, lanes]
            qs = jnp.concatenate([jnp.where(low, qp, jnp.zeros_like(qp)),
                                  jnp.where(low, jnp.zeros_like(qp), qp)], axis=0)
            bias = jnp.where(top, bias_ref[2 * p], bias_ref[2 * p + 1])
            z = _dot_nt(qs, k_ref[0, pl.ds(ks, CHUNK), lanes]) + bias
            ls = _sb_scores(z)
            if diagonal:
                ls = jnp.where(strictly_before, ls, 0.0)
            sc = _split_dot(ls, u2, 2)
            suffix, total = sc[:, :CHUNK], sc[:, CHUNK:]
            if diagonal:
                w = jnp.where(strictly_before, jnp.exp(z + suffix), 0.0)
            else:
                later = r_sc[p]
                w = jnp.exp(z + later + suffix)
            pv = _dot(w.astype(BF16), v_ref[0, pl.ds(ks, CHUNK), lanes])
            if diagonal:
                acc_sc[p] = pv
                r_sc[p] = total
            else:
                acc_sc[p] = acc_sc[p] + pv
                r_sc[p] = later + total

    block(i, True)

    def body(j, carry):
        block(i - 1 - j, False)
        return carry

    lax.fori_loop(0, i, body, 0)
    outs = []
    for p in range(N_PAIRS):
        a = acc_sc[p]
        outs.append(jnp.where(low, a[:tq], a[tq:]))
    o_ref[0] = _rms(jnp.concatenate(outs, axis=1), g_ref[...]).astype(o_ref.dtype)


def _attn(qa, ka, va, seq, layer, bias, out_norm):
    nb, seq_pad, _ = qa.shape
    nq = seq_pad // CHUNK
    return pl.pallas_call(
        _attn_body,
        grid=(nb, nq),
        in_specs=[pl.BlockSpec((1, CHUNK, WIDTH), lambda b, i: (b, i, 0)),
                  pl.BlockSpec((1, seq_pad, WIDTH), lambda b, i: (b, 0, 0)),
                  pl.BlockSpec((1, seq_pad, WIDTH), lambda b, i: (b, 0, 0)),
                  pl.BlockSpec(memory_space=pltpu.SMEM),
                  pl.BlockSpec((None, 1, WIDTH), lambda b, i: (layer, 0, 0))],
        out_specs=pl.BlockSpec((1, CHUNK, WIDTH), lambda b, i: (b, i, 0)),
        out_shape=jax.ShapeDtypeStruct((nb, seq, WIDTH), BF16),
        scratch_shapes=[pltpu.VMEM((N_PAIRS, 2 * CHUNK, CHUNK), F32),
                        pltpu.VMEM((N_PAIRS, 2 * CHUNK, LANES), F32)],
        compiler_params=pltpu.CompilerParams(dimension_semantics=("parallel", "arbitrary"),
                                             vmem_limit_bytes=VMEM_LIMIT),
        name="sb_attn",
    )(qa, ka, va, bias, out_norm)


def _decode_body(n_group, pt_ref, q_ref, kn_ref, vn_ref, bias_ref, g_ref, *rest):
    k_refs = rest[:n_group]
    v_refs = rest[n_group:2 * n_group]
    o_ref, r_sc, acc_sc = rest[2 * n_group:]
    j = pl.program_id(1)
    t_new = q_ref.shape[1]
    rows = N_HEADS * t_new
    page = CHUNK
    u2 = _suffix_matrix(page)
    t_bits, d_bits = t_new.bit_length() - 1, HEAD_DIM.bit_length() - 1
    own = (_iota((rows, WIDTH), 0) >> t_bits) == (_iota((rows, WIDTH), 1) >> d_bits)
    q = q_ref[0]
    qbd = jnp.where(own, jnp.concatenate([q] * N_HEADS, axis=0), 0.0).astype(BF16)
    bias = bias_ref[...]

    @pl.when(j == 0)
    def _():
        pad = jnp.zeros((page - t_new, WIDTH), F32)
        kb = jnp.concatenate([kn_ref[0], pad], axis=0).astype(BF16)
        vb = jnp.concatenate([vn_ref[0], pad], axis=0).astype(BF16)
        z = _dot_nt(qbd, kb) + bias
        visible = _iota((rows, page), 1) < (_iota((rows, page), 0) & (t_new - 1))
        ls = jnp.where(visible, _sb_scores(z), 0.0)
        sc = _split_dot(ls, u2, 2)
        w = jnp.where(visible, jnp.exp(z + sc[:, :page]), 0.0)
        acc_sc[...] = _dot(w.astype(BF16), vb)
        r_sc[...] = sc[:, page:]

    zs, sufs, tots = [], [], []
    for g in range(n_group):
        z = _dot_nt(qbd, k_refs[g][0].astype(BF16)) + bias
        sc = _split_dot(_sb_scores(z), u2, 2)
        zs.append(z)
        sufs.append(sc[:, :page])
        tots.append(sc[:, page:])
    later = r_sc[...]
    acc = acc_sc[...]
    for g in range(n_group):
        w = jnp.exp(zs[g] + later + sufs[g])
        acc = acc + _dot(w.astype(BF16), v_refs[g][0].astype(BF16))
        later = later + tots[g]
    acc_sc[...] = acc
    r_sc[...] = later

    @pl.when(j == pl.num_programs(1) - 1)
    def _():
        a = jnp.where(own, acc, 0.0)
        o = a[0:t_new]
        for h in range(1, N_HEADS):
            o = o + a[h * t_new:(h + 1) * t_new]
        o_ref[0] = _rms(o, g_ref[...]).astype(o_ref.dtype)


def _decode(q, k_new, v_new, cache_k, cache_v, page_table, layer, n_phys, bias_tile, out_norm):
    db, t_new, _ = q.shape
    n_pages = page_table.shape[1]
    page = cache_k.shape[1]
    assert page == CHUNK and t_new == 8
    n_group = _pick_tile(n_pages, 8, 1)
    rows = N_HEADS * t_new
    per_b = lambda b, j, pt: (b, 0, 0)

    def page_spec(g):
        return pl.BlockSpec((1, page, WIDTH),
                            lambda b, j, pt: (layer * n_phys + pt[b, n_pages - 1 - (j * n_group + g)], 0, 0))

    grid_spec = pltpu.PrefetchScalarGridSpec(
        num_scalar_prefetch=1,
        grid=(db, n_pages // n_group),
        in_specs=[pl.BlockSpec((1, t_new, WIDTH), per_b), pl.BlockSpec((1, t_new, WIDTH), per_b),
                  pl.BlockSpec((1, t_new, WIDTH), per_b),
                  pl.BlockSpec((None, rows, LANES), lambda b, j, pt: (layer, 0, 0)),
                  pl.BlockSpec((None, 1, WIDTH), lambda b, j, pt: (layer, 0, 0))]
                 + [page_spec(g) for g in range(n_group)] * 2,
        out_specs=pl.BlockSpec((1, t_new, WIDTH), per_b),
        scratch_shapes=[pltpu.VMEM((rows, LANES), F32), pltpu.VMEM((rows, WIDTH), F32)],
    )
    return pl.pallas_call(
        functools.partial(_decode_body, n_group),
        grid_spec=grid_spec,
        out_shape=jax.ShapeDtypeStruct((db, t_new, WIDTH), F32),
        compiler_params=pltpu.CompilerParams(dimension_semantics=("parallel", "arbitrary"),
                                             vmem_limit_bytes=VMEM_LIMIT),
        name="sb_decode",
    )(page_table, q, k_new, v_new, bias_tile, out_norm, *([cache_k] * n_group), *([cache_v] * n_group))


def kernel(x_prompt, x_sample, cache_k, cache_v, state_ssm, state_conv, page_table, meta_tokens,
           norm_ffn1, ffn1_w_gu, ffn1_w_down, norm_mix, w_in, conv_w, conv_b, dt_bias, A_log,
           D_skip, ssd_norm, q_norm, k_norm, sb_bias, sb_out_norm, w_out, norm_ffn2, ffn2_w_gu,
           ffn2_w_down):
    bp, seq_in, d = x_prompt.shape
    db, t_new, _ = x_sample.shape
    depth = w_in.shape[0]
    n_meta = meta_tokens.shape[0]
    n_phys, page = cache_k.shape[1], cache_k.shape[2]
    seq = n_meta + seq_in
    seq_pad = pl.cdiv(seq, CHUNK) * CHUNK

    wgu1, wd1 = ffn1_w_gu.astype(BF16), ffn1_w_down.astype(BF16)
    wgu2, wd2 = ffn2_w_gu.astype(BF16), ffn2_w_down.astype(BF16)
    wo = w_out.astype(BF16)
    o_z, o_xbc, o_dt = WIDTH, WIDTH + CONV_DIM, WIDTH + CONV_DIM + N_HEADS
    w_dt = jnp.pad(w_in[:, :, o_xbc:o_dt], ((0, 0), (0, 0), (0, DT_PAD - N_HEADS)))
    w_in_r = jnp.concatenate([w_in[:, :, :o_xbc], w_in[:, :, o_dt:], w_dt], axis=-1).astype(BF16)
    assert w_in_r.shape[-1] == _C_END
    vec = lambda a: a[:, None, :]
    norm_ffn1, norm_mix, norm_ffn2 = vec(norm_ffn1), vec(norm_mix), vec(norm_ffn2)
    qn = vec(jnp.tile(q_norm, (1, N_HEADS)))
    kn = vec(jnp.tile(k_norm, (1, N_HEADS)))
    ones_bd = ((jnp.arange(WIDTH)[:, None] // HEAD_DIM) == (jnp.arange(WIDTH)[None, :] // HEAD_DIM)).astype(BF16)
    expand = (jnp.arange(LANES)[:, None] == (jnp.arange(WIDTH)[None, :] // HEAD_DIM)).astype(BF16)
    cw = jnp.pad(conv_w, ((0, 0), (0, 8 - CONV_W), (0, 0)))
    conv_b = vec(conv_b)
    dtb = vec(jnp.pad(dt_bias, ((0, 0), (0, DT_PAD - N_HEADS))))
    alog = vec(jnp.pad(A_log, ((0, 0), (0, DT_PAD - N_HEADS))))
    dsk = vec(jnp.repeat(D_skip, HEAD_DIM, axis=1))
    ssd_norm, sb_out_norm = vec(ssd_norm), vec(sb_out_norm)
    bias_tile = jnp.broadcast_to(jnp.repeat(sb_bias, t_new, axis=1)[:, :, None], (depth, N_HEADS * t_new, LANES))
    ck = cache_k.reshape(depth * n_phys, page, WIDTH)
    cv = cache_v.reshape(depth * n_phys, page, WIDTH)

    meta = jnp.broadcast_to(meta_tokens[None].astype(x_prompt.dtype), (bp, n_meta, d))
    xp = jnp.concatenate([meta, x_prompt], axis=1).reshape(bp * seq, d)
    xs = x_sample.reshape(db * t_new, d)
    conv0_p = jnp.zeros((bp, 8, CONV_DIM), F32)
    state0_p = jnp.zeros((bp, LANES, WIDTH), F32)

    outs = {name: [] for name in ("kp", "vp", "sp", "cp", "ks", "vs", "ss", "cs")}
    for l in range(depth):
        xp = _ffn(xp, l, norm_ffn1, wgu1, wd1)
        z, xbc, dt, k, v, qa, ka, va = _mixin(xp.reshape(bp, seq, d), l, norm_mix, w_in_r, qn, kn, ones_bd, seq_pad)
        y_ssd, s_fin = _ssd(xbc, z, dt, seq, conv0_p, state0_p, l, cw, conv_b, dtb, alog, dsk, ssd_norm, expand)
        y_sb = _attn(qa, ka, va, seq, l, sb_bias[l], sb_out_norm)
        xp = _ffn(xp, l, norm_ffn2, wgu2, wd2,
                  mix=(y_ssd.reshape(bp * seq, WIDTH), y_sb.reshape(bp * seq, WIDTH), wo))
        outs["kp"].append(k.reshape(bp, seq, N_HEADS, HEAD_DIM))
        outs["vp"].append(v.reshape(bp, seq, N_HEADS, HEAD_DIM))
        outs["sp"].append(_blocks_to_state(s_fin))
        outs["cp"].append(xbc[:, seq - (CONV_W - 1):, :])

        xs = _ffn(xs, l, norm_ffn1, wgu1, wd1)
        z, xbc, dt, k, v, qa, _, _ = _mixin(xs.reshape(1, db * t_new, d), l, norm_mix, w_in_r, qn, kn, ones_bd,
                                            db * t_new)
        shp = lambda a: a.reshape(db, t_new, a.shape[-1])
        xbc, k, v = shp(xbc), shp(k), shp(v)
        chunk = lambda a: jnp.pad(shp(a), ((0, 0), (0, CHUNK - t_new), (0, 0)))
        conv0_s = jnp.pad(state_conv[l], ((0, 0), (8 - (CONV_W - 1), 0), (0, 0)))
        y_ssd, s_fin = _ssd(chunk(xbc), chunk(z), chunk(dt), t_new, conv0_s, _state_to_blocks(state_ssm[l]), l,
                            cw, conv_b, dtb, alog, dsk, ssd_norm, expand)
        y_ssd = y_ssd[:, :t_new]
        y_sb = _decode(shp(qa).astype(F32), k, v, ck, cv, page_table, l, n_phys, bias_tile, sb_out_norm)
        xs = _ffn(xs, l, norm_ffn2, wgu2, wd2,
                  mix=(y_ssd.reshape(db * t_new, WIDTH), y_sb.reshape(db * t_new, WIDTH), wo))
        outs["ks"].append(k.reshape(db, t_new, N_HEADS, HEAD_DIM))
        outs["vs"].append(v.reshape(db, t_new, N_HEADS, HEAD_DIM))
        outs["ss"].append(_blocks_to_state(s_fin))
        outs["cs"].append(jnp.concatenate([state_conv[l], xbc], axis=1)[:, -(CONV_W - 1):, :])

    y_prompt = xp.reshape(bp, seq, d)[:, n_meta:]
    y_sample = xs.reshape(db, t_new, d)
    st = lambda name: jnp.stack(outs[name])
    return (y_prompt, y_sample, st("kp"), st("vp"), st("sp"), st("cp"),
            st("ks"), st("vs"), st("ss"), st("cs"))
```

```python
import functools

import jax
import jax.numpy as jnp
from jax import lax
from jax.experimental import pallas as pl
from jax.experimental.pallas import tpu as pltpu

F32 = jnp.float32
BF16 = jnp.bfloat16
EPS = 1e-6
LOG2E = 1.4426950408889634

LANES = 128
HEAD_DIM = 64
N_HEADS = 8
WIDTH = N_HEADS * HEAD_DIM
N_PAIRS = WIDTH // LANES
SSD_GROUPS = 2
CONV_W = 4
CONV_DIM = WIDTH + 2 * SSD_GROUPS * HEAD_DIM
DT_PAD = LANES
CHUNK = 128
KEY_TILE = 256
VMEM_LIMIT = 56 * 1024 * 1024


def _dot(a, b):
    return jnp.dot(a, b, preferred_element_type=F32)


def _dot_nt(a, b):
    return lax.dot_general(a, b, (((1,), (1,)), ((), ())), preferred_element_type=F32)


def _dot_tn(a, b):
    return lax.dot_general(a, b, (((0,), (0,)), ((), ())), preferred_element_type=F32)


def _split(a, passes):
    parts = []
    rem = a
    for p in range(passes):
        part = rem.astype(BF16)
        parts.append(part)
        if p + 1 < passes:
            rem = rem - part.astype(F32)
    return parts


def _split_dot(a, m, passes):
    out = None
    for part in _split(a, passes):
        term = _dot(part, m)
        out = term if out is None else out + term
    return out


def _rms(x, g):
    ms = jnp.mean(x * x, axis=-1, keepdims=True)
    return x * lax.rsqrt(ms + EPS) * g


def _silu(x):
    return x * jax.nn.sigmoid(x)


def _softplus(x):
    return jnp.maximum(x, 0.0) + jnp.log1p(jnp.exp(-jnp.abs(x)))


def _iota(shape, dim):
    return lax.broadcasted_iota(jnp.int32, shape, dim)


def _pick_tile(n, cap, mult):
    if n <= cap:
        return n
    best = None
    for t in range(mult, cap + 1, mult):
        if n % t == 0:
            best = t
    assert best is not None, (n, cap, mult)
    return best


def _resident(shape, index_map):
    return pl.BlockSpec(shape, index_map, pipeline_mode=pl.Buffered(1))


def _ffn_body(has_mix, n_chunks, fc, *refs):
    if has_mix:
        x_ref, ys_ref, yb_ref, wo_ref, g_ref, wgu_ref, wd_ref, o_ref = refs
        x = (x_ref[...]
             + _dot(ys_ref[...].astype(BF16), wo_ref[:WIDTH, :])
             + _dot(yb_ref[...].astype(BF16), wo_ref[WIDTH:, :]))
    else:
        x_ref, g_ref, wgu_ref, wd_ref, o_ref = refs
        x = x_ref[...]
    h = _rms(x, g_ref[...]).astype(BF16)
    d_ff = n_chunks * fc
    acc = None
    for c in range(n_chunks):
        gate = _dot(h, wgu_ref[:, c * fc:(c + 1) * fc])
        up = _dot(h, wgu_ref[:, d_ff + c * fc:d_ff + (c + 1) * fc])
        a = (_silu(gate) * up).astype(BF16)
        d = _dot(a, wd_ref[c * fc:(c + 1) * fc, :])
        acc = d if acc is None else acc + d
    o_ref[...] = x + 0.5 * acc


def _ffn(x, layer, norm, wgu, wd, mix=None):
    t, d = x.shape
    d_ff = wd.shape[1]
    fc = 256
    assert d_ff % fc == 0
    tm = _pick_tile(t, 704, 16)
    row = lambda i: (i, 0)
    lay3 = lambda i: (layer, 0, 0)
    in_specs = [pl.BlockSpec((tm, d), row)]
    args = [x]
    if mix is not None:
        ys, yb, wo = mix
        in_specs += [pl.BlockSpec((tm, WIDTH), row), pl.BlockSpec((tm, WIDTH), row),
                     _resident((None, 2 * WIDTH, d), lay3)]
        args += [ys, yb, wo]
    in_specs += [_resident((None, 1, d), lay3), _resident((None, d, 2 * d_ff), lay3),
                 _resident((None, d_ff, d), lay3)]
    args += [norm, wgu, wd]
    return pl.pallas_call(
        functools.partial(_ffn_body, mix is not None, d_ff // fc, fc),
        grid=(t // tm,),
        in_specs=in_specs,
        out_specs=pl.BlockSpec((tm, d), row),
        out_shape=jax.ShapeDtypeStruct((t, d), F32),
        compiler_params=pltpu.CompilerParams(dimension_semantics=("parallel",), vmem_limit_bytes=VMEM_LIMIT),
        name="ffn_mix" if mix is not None else "ffn",
    )(*args)


_C_Z, _C_XBC, _C_Q, _C_K, _C_V, _C_DT, _C_END = 0, 512, 1280, 1792, 2304, 2816, 2944


def _head_norm(t, gain, ones_bd):
    ms = _split_dot(t * t, ones_bd, 2) * (1.0 / HEAD_DIM)
    return t * lax.rsqrt(ms + EPS) * gain


def _mixin_body(seq, tl, x_ref, g_ref, w_ref, qn_ref, kn_ref, bd_ref,
                z_ref, xbc_ref, dt_ref, k_ref, v_ref, qa_ref, ka_ref, va_ref):
    j = pl.program_id(1)
    h = _rms(x_ref[0], g_ref[...]).astype(BF16)
    p = _dot(h, w_ref[...])
    z_ref[0] = p[:, _C_Z:_C_XBC]
    xbc_ref[0] = p[:, _C_XBC:_C_Q]
    dt_ref[0] = p[:, _C_DT:_C_END]
    bd = bd_ref[...]
    q = _head_norm(p[:, _C_Q:_C_K], qn_ref[...], bd)
    k = _head_norm(p[:, _C_K:_C_V], kn_ref[...], bd)
    v = p[:, _C_V:_C_DT]
    k_ref[0] = k
    v_ref[0] = v
    valid = (j * tl + _iota((tl, 1), 0)) < seq
    qa_ref[0] = jnp.where(valid, q * (HEAD_DIM ** -0.5), 0.0).astype(BF16)
    ka_ref[0] = jnp.where(valid, k, 0.0).astype(BF16)
    va_ref[0] = jnp.where(valid, v, 0.0).astype(BF16)


def _mixin(x, layer, norm, w_in, qn, kn, ones_bd, seq_pad):
    nb, seq, d = x.shape
    tl = _pick_tile(seq_pad, 640, 16)
    blk = lambda b, j: (b, j, 0)
    lay3 = lambda b, j: (layer, 0, 0)
    f32_out = lambda c: jax.ShapeDtypeStruct((nb, seq, c), F32)
    att_out = jax.ShapeDtypeStruct((nb, seq_pad, WIDTH), BF16)
    return pl.pallas_call(
        functools.partial(_mixin_body, seq, tl),
        grid=(nb, seq_pad // tl),
        in_specs=[pl.BlockSpec((1, tl, d), blk), _resident((None, 1, d), lay3),
                  _resident((None, d, _C_END), lay3), _resident((None, 1, WIDTH), lay3),
                  _resident((None, 1, WIDTH), lay3), _resident((WIDTH, WIDTH), lambda b, j: (0, 0))],
        out_specs=[pl.BlockSpec((1, tl, WIDTH), blk), pl.BlockSpec((1, tl, CONV_DIM), blk),
                   pl.BlockSpec((1, tl, DT_PAD), blk), pl.BlockSpec((1, tl, WIDTH), blk),
                   pl.BlockSpec((1, tl, WIDTH), blk), pl.BlockSpec((1, tl, WIDTH), blk),
                   pl.BlockSpec((1, tl, WIDTH), blk), pl.BlockSpec((1, tl, WIDTH), blk)],
        out_shape=[f32_out(WIDTH), f32_out(CONV_DIM), f32_out(DT_PAD), f32_out(WIDTH), f32_out(WIDTH),
                   att_out, att_out, att_out],
        compiler_params=pltpu.CompilerParams(dimension_semantics=("parallel", "parallel"),
                                             vmem_limit_bytes=VMEM_LIMIT),
        name="mix_in",
    )(x, norm, w_in, qn, kn, ones_bd)


def _ssd_body(seq, q, xbc_ref, z_ref, dt_ref, cinit_ref, sinit_ref, cw_ref, cb_ref, dtb_ref, alog_ref,
              dsk_ref, nw_ref, exp_ref, y_ref, sout_ref, cbuf, st):
    c = pl.program_id(1)

    @pl.when(c == 0)
    def _():
        cbuf[0:8, :] = cinit_ref[0]
        st[...] = sinit_ref[0]

    valid = (c * q + _iota((q, 1), 0)) < seq
    xraw = jnp.where(valid, xbc_ref[0], 0.0)
    cbuf[8:8 + q, :] = xraw
    cw = cw_ref[...]
    conv = (cbuf[5:5 + q, :] * cw[0:1] + cbuf[6:6 + q, :] * cw[1:2]
            + cbuf[7:7 + q, :] * cw[2:3] + xraw * cw[3:4])
    cbuf[5:8, :] = cbuf[5 + q:8 + q, :]
    xc = _silu(conv + cb_ref[...])
    xs = xc[:, :WIDTH]
    bm = xc[:, WIDTH:WIDTH + LANES].astype(BF16)
    cm = xc[:, WIDTH + LANES:].astype(BF16)

    dt = jnp.where(valid, _softplus(dt_ref[0] + dtb_ref[...]), 0.0)
    a = dt * (-jnp.exp(alog_ref[...]))
    tri = jnp.where(_iota((q, q), 0) >= _iota((q, q), 1), 1.0, 0.0).astype(BF16)
    cs = None
    for part in _split(a, 3):
        term = _dot(tri, part)
        cs = term if cs is None else cs + term
    ecs = jnp.exp(cs)
    to_end = jnp.exp(cs[q - 1:q, :] - cs)
    expand = exp_ref[...]
    dt_e = _split_dot(dt, expand, 2)
    ecs_e = _split_dot(ecs, expand, 2)
    to_end_e = _split_dot(to_end, expand, 2)
    xdt = xs * dt_e

    lane = _iota((q, LANES), 1)
    low = lane < HEAD_DIM
    g0 = _dot_nt(jnp.where(low, cm, jnp.zeros_like(cm)), bm)
    g1 = _dot_nt(jnp.where(low, jnp.zeros_like(cm), cm), bm)
    cs_t = cs.T
    causal = _iota((q, q), 0) >= _iota((q, q), 1)
    xdt_b = xdt.astype(BF16)
    y_pairs = []
    for p in range(N_PAIRS):
        xp = xdt_b[:, p * LANES:(p + 1) * LANES]
        yp = None
        for h, xm in ((2 * p, jnp.where(low, xp, jnp.zeros_like(xp))),
                      (2 * p + 1, jnp.where(low, jnp.zeros_like(xp), xp))):
            seg = cs[:, h:h + 1] - cs_t[h:h + 1, :]
            decay = jnp.exp(jnp.where(causal, seg, -jnp.inf))
            m = ((g0 if h < N_HEADS // SSD_GROUPS else g1) * decay).astype(BF16)
            t = _dot(m, xm)
            yp = t if yp is None else yp + t
        y_pairs.append(yp)
    y_diag = jnp.concatenate(y_pairs, axis=1)

    s_prev = st[...]
    y_off = _dot(cm, s_prev.astype(BF16)) * ecs_e
    y = y_diag + y_off + dsk_ref[...] * xs
    zz = jnp.where(valid, z_ref[0], 0.0)
    y = y * _silu(zz)
    half = WIDTH // SSD_GROUPS
    y0, y1 = y[:, :half], y[:, half:]
    y0 = y0 * lax.rsqrt(jnp.mean(y0 * y0, axis=-1, keepdims=True) + EPS)
    y1 = y1 * lax.rsqrt(jnp.mean(y1 * y1, axis=-1, keepdims=True) + EPS)
    y_ref[0] = (jnp.concatenate([y0, y1], axis=1) * nw_ref[...]).astype(y_ref.dtype)

    upd = _dot_tn(bm, (xdt * to_end_e).astype(BF16))
    own = (_iota((LANES, WIDTH), 0) < HEAD_DIM) == (_iota((LANES, WIDTH), 1) < half)
    st[...] = s_prev * ecs_e[q - 1:q, :] + jnp.where(own, upd, 0.0)

    @pl.when(c == pl.num_programs(1) - 1)
    def _():
        sout_ref[0] = st[...]


def _ssd(xbc, z, dt, seq, conv_init, state_init, layer, cw, cb, dtb, alog, dsk, nw, expand):
    nb, n_rows, _ = xbc.shape
    q = CHUNK
    nchunk = pl.cdiv(n_rows, q)
    blk = lambda b, c: (b, c, 0)
    per_b = lambda b, c: (b, 0, 0)
    lay3 = lambda b, c: (layer, 0, 0)
    return pl.pallas_call(
        functools.partial(_ssd_body, seq, q),
        grid=(nb, nchunk),
        in_specs=[pl.BlockSpec((1, q, CONV_DIM), blk), pl.BlockSpec((1, q, WIDTH), blk),
                  pl.BlockSpec((1, q, DT_PAD), blk), pl.BlockSpec((1, 8, CONV_DIM), per_b),
                  pl.BlockSpec((1, LANES, WIDTH), per_b), pl.BlockSpec((None, 8, CONV_DIM), lay3),
                  pl.BlockSpec((None, 1, CONV_DIM), lay3), pl.BlockSpec((None, 1, DT_PAD), lay3),
                  pl.BlockSpec((None, 1, DT_PAD), lay3), pl.BlockSpec((None, 1, WIDTH), lay3),
                  pl.BlockSpec((None, 1, WIDTH), lay3), pl.BlockSpec((LANES, WIDTH), lambda b, c: (0, 0))],
        out_specs=[pl.BlockSpec((1, q, WIDTH), blk), pl.BlockSpec((1, LANES, WIDTH), per_b)],
        out_shape=[jax.ShapeDtypeStruct((nb, n_rows, WIDTH), BF16),
                   jax.ShapeDtypeStruct((nb, LANES, WIDTH), F32)],
        scratch_shapes=[pltpu.VMEM((8 + q, CONV_DIM), F32), pltpu.VMEM((LANES, WIDTH), F32)],
        compiler_params=pltpu.CompilerParams(dimension_semantics=("parallel", "arbitrary"),
                                             vmem_limit_bytes=VMEM_LIMIT),
        name="ssd",
    )(xbc, z, dt, conv_init, state_init, cw, cb, dtb, alog, dsk, nw, expand)


def _state_to_blocks(s):
    nb = s.shape[0]
    hpg = N_HEADS // SSD_GROUPS
    t = jnp.transpose(s.reshape(nb, SSD_GROUPS, hpg, HEAD_DIM, HEAD_DIM), (0, 1, 4, 2, 3))
    t = t.reshape(nb, SSD_GROUPS, HEAD_DIM, hpg * HEAD_DIM)
    zero = jnp.zeros_like(t[:, 0])
    top = jnp.concatenate([t[:, 0], zero], axis=-1)
    bot = jnp.concatenate([zero, t[:, 1]], axis=-1)
    return jnp.concatenate([top, bot], axis=1)


def _blocks_to_state(sb):
    nb = sb.shape[0]
    hpg = N_HEADS // SSD_GROUPS
    half = WIDTH // SSD_GROUPS
    t = jnp.stack([sb[:, :HEAD_DIM, :half], sb[:, HEAD_DIM:, half:]], axis=1)
    t = t.reshape(nb, SSD_GROUPS, HEAD_DIM, hpg, HEAD_DIM)
    return jnp.transpose(t, (0, 1, 3, 4, 2)).reshape(nb, N_HEADS, HEAD_DIM, HEAD_DIM)


def _sb_softplus(z):
    return jnp.maximum(z, 0.0) + jnp.log(1.0 + jnp.exp2(jnp.abs(z) * (-LOG2E)))


def _suffix_matrix(tk):
    return jnp.where(_iota((tk, tk), 0) >= _iota((tk, tk), 1), 1.0, 0.0).astype(BF16)


def _attn_body(q_ref, k_ref, v_ref, bias_ref, g_ref, o_ref, qm_sc, r_sc, acc_sc):
    i = pl.program_id(1)
    tq, tk = CHUNK, KEY_TILE
    low = _iota((tq, LANES), 1) < HEAD_DIM
    for p in range(N_PAIRS):
        qp = q_ref[0, :, p * LANES:(p + 1) * LANES]
        qm_sc[2 * p] = jnp.where(low, qp, jnp.zeros_like(qp))
        qm_sc[2 * p + 1] = jnp.where(low, jnp.zeros_like(qp), qp)
    u = _suffix_matrix(tk)
    kt_diag = i // 2
    visible = _iota((tq, tk), 1) < _iota((tq, tk), 0) + (i - 2 * kt_diag) * tq

    def tile(kt, diagonal):
        ks = pl.multiple_of(kt * tk, tk)
        pair = lambda h: slice((h // 2) * LANES, (h // 2 + 1) * LANES)
        heads = range(N_HEADS)
        zs = [_dot_nt(qm_sc[h], k_ref[0, pl.ds(ks, tk), pair(h)]) + bias_ref[h] for h in heads]
        ps = [_sb_softplus(z) for z in zs]
        if diagonal:
            ps = [jnp.where(visible, p, 0.0) for p in ps]
        parts = [_split(p, 2) for p in ps]
        sufs = [_dot(hi, u) + _dot(lo, u) for hi, lo in parts]
        ws = [jnp.exp(z - s) for z, s in zip(zs, sufs)]
        if diagonal:
            ws = [jnp.where(visible, w, 0.0) for w in ws]
        pvs = [_dot(ws[h].astype(BF16), v_ref[0, pl.ds(ks, tk), pair(h)]) for h in heads]
        for h in heads:
            total = jnp.sum(ps[h], axis=1, keepdims=True)
            if diagonal:
                acc_sc[h] = pvs[h]
                r_sc[h] = jnp.broadcast_to(total, (tq, LANES))
            else:
                later = r_sc[h]
                acc_sc[h] = acc_sc[h] + jnp.exp(-later) * pvs[h]
                r_sc[h] = later + total

    tile(kt_diag, True)

    def body(j, carry):
        tile(kt_diag - 1 - j, False)
        return carry

    lax.fori_loop(0, kt_diag, body, 0)
    outs = [jnp.where(low, acc_sc[2 * p], acc_sc[2 * p + 1]) for p in range(N_PAIRS)]
    o_ref[0] = _rms(jnp.concatenate(outs, axis=1), g_ref[...]).astype(o_ref.dtype)


def _attn(qa, ka, va, seq, layer, bias, out_norm):
    nb, seq_pad, _ = qa.shape
    assert seq_pad % KEY_TILE == 0
    nq = pl.cdiv(seq, CHUNK)
    head_tile = lambda: pltpu.VMEM((N_HEADS, CHUNK, LANES), F32)
    return pl.pallas_call(
        _attn_body,
        grid=(nb, nq),
        in_specs=[pl.BlockSpec((1, CHUNK, WIDTH), lambda b, i: (b, i, 0)),
                  pl.BlockSpec((1, seq_pad, WIDTH), lambda b, i: (b, 0, 0)),
                  pl.BlockSpec((1, seq_pad, WIDTH), lambda b, i: (b, 0, 0)),
                  pl.BlockSpec(memory_space=pltpu.SMEM),
                  pl.BlockSpec((None, 1, WIDTH), lambda b, i: (layer, 0, 0))],
        out_specs=pl.BlockSpec((1, CHUNK, WIDTH), lambda b, i: (b, i, 0)),
        out_shape=jax.ShapeDtypeStruct((nb, seq, WIDTH), BF16),
        scratch_shapes=[pltpu.VMEM((N_HEADS, CHUNK, LANES), BF16), head_tile(), head_tile()],
        compiler_params=pltpu.CompilerParams(dimension_semantics=("parallel", "arbitrary"),
                                             vmem_limit_bytes=VMEM_LIMIT),
        name="sb_attn",
    )(qa, ka, va, bias, out_norm)


def _decode_body(n_group, pt_ref, q_ref, kn_ref, vn_ref, bias_ref, g_ref, *rest):
    k_refs = rest[:n_group]
    v_refs = rest[n_group:2 * n_group]
    o_ref, r_sc, acc_sc = rest[2 * n_group:]
    j = pl.program_id(1)
    t_new = q_ref.shape[1]
    rows = N_HEADS * t_new
    page = CHUNK
    u = _suffix_matrix(page)
    t_bits, d_bits = t_new.bit_length() - 1, HEAD_DIM.bit_length() - 1
    own = (_iota((rows, WIDTH), 0) >> t_bits) == (_iota((rows, WIDTH), 1) >> d_bits)
    q = q_ref[0]
    qbd = jnp.where(own, jnp.concatenate([q] * N_HEADS, axis=0), 0.0).astype(BF16)
    bias = bias_ref[...]

    @pl.when(j == 0)
    def _():
        pad = jnp.zeros((page - t_new, WIDTH), F32)
        kb = jnp.concatenate([kn_ref[0], pad], axis=0).astype(BF16)
        vb = jnp.concatenate([vn_ref[0], pad], axis=0).astype(BF16)
        z = _dot_nt(qbd, kb) + bias
        visible = _iota((rows, page), 1) < (_iota((rows, page), 0) & (t_new - 1))
        p = jnp.where(visible, _sb_softplus(z), 0.0)
        w = jnp.where(visible, jnp.exp(z - _split_dot(p, u, 2)), 0.0)
        acc_sc[...] = _dot(w.astype(BF16), vb)
        r_sc[...] = jnp.broadcast_to(jnp.sum(p, axis=1, keepdims=True), (rows, LANES))

    pages = range(n_group)
    zs = [_dot_nt(qbd, k_refs[g][0]) + bias for g in pages]
    ps = [_sb_softplus(z) for z in zs]
    parts = [_split(p, 2) for p in ps]
    sufs = [_dot(hi, u) + _dot(lo, u) for hi, lo in parts]
    later = r_sc[...]
    ws = []
    for g in pages:
        ws.append(jnp.exp(zs[g] - later - sufs[g]).astype(BF16))
        later = later + jnp.sum(ps[g], axis=1, keepdims=True)
    acc = acc_sc[...]
    for g in pages:
        acc = acc + _dot(ws[g], v_refs[g][0])
    acc_sc[...] = acc
    r_sc[...] = later

    @pl.when(j == pl.num_programs(1) - 1)
    def _():
        a = jnp.where(own, acc, 0.0)
        o = a[0:t_new]
        for h in range(1, N_HEADS):
            o = o + a[h * t_new:(h + 1) * t_new]
        o_ref[0] = _rms(o, g_ref[...]).astype(o_ref.dtype)


def _decode(q, k_new, v_new, cache_k, cache_v, page_table, layer, n_phys, bias_tile, out_norm):
    db, t_new, _ = q.shape
    n_pages = page_table.shape[1]
    page = cache_k.shape[1]
    assert page == CHUNK and t_new == 8
    n_group = _pick_tile(n_pages, 8, 1)
    rows = N_HEADS * t_new
    per_b = lambda b, j, pt: (b, 0, 0)

    def page_spec(g):
        return pl.BlockSpec((1, page, WIDTH),
                            lambda b, j, pt: (layer * n_phys + pt[b, n_pages - 1 - (j * n_group + g)], 0, 0))

    grid_spec = pltpu.PrefetchScalarGridSpec(
        num_scalar_prefetch=1,
        grid=(db, n_pages // n_group),
        in_specs=[pl.BlockSpec((1, t_new, WIDTH), per_b), pl.BlockSpec((1, t_new, WIDTH), per_b),
                  pl.BlockSpec((1, t_new, WIDTH), per_b),
                  pl.BlockSpec((None, rows, LANES), lambda b, j, pt: (layer, 0, 0)),
                  pl.BlockSpec((None, 1, WIDTH), lambda b, j, pt: (layer, 0, 0))]
                 + [page_spec(g) for g in range(n_group)] * 2,
        out_specs=pl.BlockSpec((1, t_new, WIDTH), per_b),
        scratch_shapes=[pltpu.VMEM((rows, LANES), F32), pltpu.VMEM((rows, WIDTH), F32)],
    )
    return pl.pallas_call(
        functools.partial(_decode_body, n_group),
        grid_spec=grid_spec,
        out_shape=jax.ShapeDtypeStruct((db, t_new, WIDTH), F32),
        compiler_params=pltpu.CompilerParams(dimension_semantics=("parallel", "arbitrary"),
                                             vmem_limit_bytes=VMEM_LIMIT),
        name="sb_decode",
    )(page_table, q, k_new, v_new, bias_tile, out_norm, *([cache_k] * n_group), *([cache_v] * n_group))


def kernel(x_prompt, x_sample, cache_k, cache_v, state_ssm, state_conv, page_table, meta_tokens,
           norm_ffn1, ffn1_w_gu, ffn1_w_down, norm_mix, w_in, conv_w, conv_b, dt_bias, A_log,
           D_skip, ssd_norm, q_norm, k_norm, sb_bias, sb_out_norm, w_out, norm_ffn2, ffn2_w_gu,
           ffn2_w_down):
    bp, seq_in, d = x_prompt.shape
    db, t_new, _ = x_sample.shape
    depth = w_in.shape[0]
    n_meta = meta_tokens.shape[0]
    n_phys, page = cache_k.shape[1], cache_k.shape[2]
    seq = n_meta + seq_in
    seq_pad = pl.cdiv(seq, KEY_TILE) * KEY_TILE

    wgu1, wd1 = ffn1_w_gu.astype(BF16), ffn1_w_down.astype(BF16)
    wgu2, wd2 = ffn2_w_gu.astype(BF16), ffn2_w_down.astype(BF16)
    wo = w_out.astype(BF16)
    o_xbc, o_dt = WIDTH + CONV_DIM, WIDTH + CONV_DIM + N_HEADS
    w_dt = jnp.pad(w_in[:, :, o_xbc:o_dt], ((0, 0), (0, 0), (0, DT_PAD - N_HEADS)))
    w_in_r = jnp.concatenate([w_in[:, :, :o_xbc], w_in[:, :, o_dt:], w_dt], axis=-1).astype(BF16)
    assert w_in_r.shape[-1] == _C_END
    vec = lambda a: a[:, None, :]
    norm_ffn1, norm_mix, norm_ffn2 = vec(norm_ffn1), vec(norm_mix), vec(norm_ffn2)
    qn = vec(jnp.tile(q_norm, (1, N_HEADS)))
    kn = vec(jnp.tile(k_norm, (1, N_HEADS)))
    ones_bd = ((jnp.arange(WIDTH)[:, None] // HEAD_DIM) == (jnp.arange(WIDTH)[None, :] // HEAD_DIM)).astype(BF16)
    expand = (jnp.arange(LANES)[:, None] == (jnp.arange(WIDTH)[None, :] // HEAD_DIM)).astype(BF16)
    cw = jnp.pad(conv_w, ((0, 0), (0, 8 - CONV_W), (0, 0)))
    conv_b = vec(conv_b)
    dtb = vec(jnp.pad(dt_bias, ((0, 0), (0, DT_PAD - N_HEADS))))
    alog = vec(jnp.pad(A_log, ((0, 0), (0, DT_PAD - N_HEADS))))
    dsk = vec(jnp.repeat(D_skip, HEAD_DIM, axis=1))
    ssd_norm, sb_out_norm = vec(ssd_norm), vec(sb_out_norm)
    bias_tile = jnp.broadcast_to(jnp.repeat(sb_bias, t_new, axis=1)[:, :, None], (depth, N_HEADS * t_new, LANES))
    ck = cache_k.astype(BF16).reshape(depth * n_phys, page, WIDTH)
    cv = cache_v.astype(BF16).reshape(depth * n_phys, page, WIDTH)

    meta = jnp.broadcast_to(meta_tokens[None].astype(x_prompt.dtype), (bp, n_meta, d))
    xp = jnp.concatenate([meta, x_prompt], axis=1).reshape(bp * seq, d)
    xs = x_sample.reshape(db * t_new, d)
    conv0_p = jnp.zeros((bp, 8, CONV_DIM), F32)
    state0_p = jnp.zeros((bp, LANES, WIDTH), F32)

    outs = {name: [] for name in ("kp", "vp", "sp", "cp", "ks", "vs", "ss", "cs")}
    for l in range(depth):
        xp = _ffn(xp, l, norm_ffn1, wgu1, wd1)
        z, xbc, dt, k, v, qa, ka, va = _mixin(xp.reshape(bp, seq, d), l, norm_mix, w_in_r, qn, kn, ones_bd, seq_pad)
        y_ssd, s_fin = _ssd(xbc, z, dt, seq, conv0_p, state0_p, l, cw, conv_b, dtb, alog, dsk, ssd_norm, expand)
        y_sb = _attn(qa, ka, va, seq, l, sb_bias[l], sb_out_norm)
        xp = _ffn(xp, l, norm_ffn2, wgu2, wd2,
                  mix=(y_ssd.reshape(bp * seq, WIDTH), y_sb.reshape(bp * seq, WIDTH), wo))
        outs["kp"].append(k.reshape(bp, seq, N_HEADS, HEAD_DIM))
        outs["vp"].append(v.reshape(bp, seq, N_HEADS, HEAD_DIM))
        outs["sp"].append(_blocks_to_state(s_fin))
        outs["cp"].append(xbc[:, seq - (CONV_W - 1):, :])

        xs = _ffn(xs, l, norm_ffn1, wgu1, wd1)
        z, xbc, dt, k, v, qa, _, _ = _mixin(xs.reshape(1, db * t_new, d), l, norm_mix, w_in_r, qn, kn, ones_bd,
                                            db * t_new)
        shp = lambda a: a.reshape(db, t_new, a.shape[-1])
        xbc, k, v = shp(xbc), shp(k), shp(v)
        chunk = lambda a: jnp.pad(shp(a), ((0, 0), (0, CHUNK - t_new), (0, 0)))
        conv0_s = jnp.pad(state_conv[l], ((0, 0), (8 - (CONV_W - 1), 0), (0, 0)))
        y_ssd, s_fin = _ssd(chunk(xbc), chunk(z), chunk(dt), t_new, conv0_s, _state_to_blocks(state_ssm[l]), l,
                            cw, conv_b, dtb, alog, dsk, ssd_norm, expand)
        y_ssd = y_ssd[:, :t_new]
        y_sb = _decode(shp(qa).astype(F32), k, v, ck, cv, page_table, l, n_phys, bias_tile, sb_out_norm)
        xs = _ffn(xs, l, norm_ffn2, wgu2, wd2,
                  mix=(y_ssd.reshape(db * t_new, WIDTH), y_sb.reshape(db * t_new, WIDTH), wo))
        outs["ks"].append(k.reshape(db, t_new, N_HEADS, HEAD_DIM))
        outs["vs"].append(v.reshape(db, t_new, N_HEADS, HEAD_DIM))
        outs["ss"].append(_blocks_to_state(s_fin))
        outs["cs"].append(jnp.concatenate([state_conv[l], xbc], axis=1)[:, -(CONV_W - 1):, :])

    y_prompt = xp.reshape(bp, seq, d)[:, n_meta:]
    y_sample = xs.reshape(db, t_new, d)
    st = lambda name: jnp.stack(outs[name])
    return (y_prompt, y_sample, st("kp"), st("vp"), st("sp"), st("cp"),
            st("ks"), st("vs"), st("ss"), st("cs"))
```

```python
import functools

import jax
import jax.numpy as jnp
from jax import lax
from jax.experimental import pallas as pl
from jax.experimental.pallas import tpu as pltpu

F32 = jnp.float32
BF16 = jnp.bfloat16
EPS = 1e-6
LOG2E = 1.4426950408889634

LANES = 128
HEAD_DIM = 64
N_HEADS = 8
WIDTH = N_HEADS * HEAD_DIM
N_PAIRS = WIDTH // LANES
SSD_GROUPS = 2
CONV_W = 4
CONV_DIM = WIDTH + 2 * SSD_GROUPS * HEAD_DIM
DT_PAD = LANES
CHUNK = 128
KEY_TILE = 256
VMEM_LIMIT = 56 * 1024 * 1024


def _dot(a, b):
    return jnp.dot(a, b, preferred_element_type=F32)


def _dot_nt(a, b):
    return lax.dot_general(a, b, (((1,), (1,)), ((), ())), preferred_element_type=F32)


def _dot_tn(a, b):
    return lax.dot_general(a, b, (((0,), (0,)), ((), ())), preferred_element_type=F32)


def _split(a, passes):
    parts = []
    rem = a
    for p in range(passes):
        part = rem.astype(BF16)
        parts.append(part)
        if p + 1 < passes:
            rem = rem - part.astype(F32)
    return parts


def _split_dot(a, m, passes):
    out = None
    for part in _split(a, passes):
        term = _dot(part, m)
        out = term if out is None else out + term
    return out


def _rms(x, g):
    ms = jnp.mean(x * x, axis=-1, keepdims=True)
    return x * lax.rsqrt(ms + EPS) * g


def _silu(x):
    return x * jax.nn.sigmoid(x)


def _softplus(x):
    return jnp.maximum(x, 0.0) + jnp.log1p(jnp.exp(-jnp.abs(x)))


def _iota(shape, dim):
    return lax.broadcasted_iota(jnp.int32, shape, dim)


def _pick_tile(n, cap, mult):
    if n <= cap:
        return n
    best = None
    for t in range(mult, cap + 1, mult):
        if n % t == 0:
            best = t
    assert best is not None, (n, cap, mult)
    return best


def _resident(shape, index_map):
    return pl.BlockSpec(shape, index_map, pipeline_mode=pl.Buffered(1))


def _ffn_body(has_mix, n_chunks, fc, *refs):
    if has_mix:
        x_ref, ys_ref, yb_ref, wo_ref, g_ref, wgu_ref, wd_ref, o_ref = refs
        x = (x_ref[...]
             + _dot(ys_ref[...].astype(BF16), wo_ref[:WIDTH, :])
             + _dot(yb_ref[...].astype(BF16), wo_ref[WIDTH:, :]))
    else:
        x_ref, g_ref, wgu_ref, wd_ref, o_ref = refs
        x = x_ref[...]
    h = _rms(x, g_ref[...]).astype(BF16)
    d_ff = n_chunks * fc
    acc = None
    for c in range(n_chunks):
        gate = _dot(h, wgu_ref[:, c * fc:(c + 1) * fc])
        up = _dot(h, wgu_ref[:, d_ff + c * fc:d_ff + (c + 1) * fc])
        a = (_silu(gate) * up).astype(BF16)
        d = _dot(a, wd_ref[c * fc:(c + 1) * fc, :])
        acc = d if acc is None else acc + d
    o_ref[...] = x + 0.5 * acc


def _ffn(x, layer, norm, wgu, wd, mix=None):
    t, d = x.shape
    d_ff = wd.shape[1]
    fc = 256
    assert d_ff % fc == 0
    tm = _pick_tile(t, 704, 16)
    row = lambda i: (i, 0)
    lay3 = lambda i: (layer, 0, 0)
    in_specs = [pl.BlockSpec((tm, d), row)]
    args = [x]
    if mix is not None:
        ys, yb, wo = mix
        in_specs += [pl.BlockSpec((tm, WIDTH), row), pl.BlockSpec((tm, WIDTH), row),
                     _resident((None, 2 * WIDTH, d), lay3)]
        args += [ys, yb, wo]
    in_specs += [_resident((None, 1, d), lay3), _resident((None, d, 2 * d_ff), lay3),
                 _resident((None, d_ff, d), lay3)]
    args += [norm, wgu, wd]
    return pl.pallas_call(
        functools.partial(_ffn_body, mix is not None, d_ff // fc, fc),
        grid=(t // tm,),
        in_specs=in_specs,
        out_specs=pl.BlockSpec((tm, d), row),
        out_shape=jax.ShapeDtypeStruct((t, d), F32),
        compiler_params=pltpu.CompilerParams(dimension_semantics=("parallel",), vmem_limit_bytes=VMEM_LIMIT),
        name="ffn_mix" if mix is not None else "ffn",
    )(*args)


_C_Z, _C_XBC, _C_Q, _C_K, _C_V, _C_DT, _C_END = 0, 512, 1280, 1792, 2304, 2816, 2944


def _head_norm(t, gain, ones_bd):
    ms = _split_dot(t * t, ones_bd, 2) * (1.0 / HEAD_DIM)
    return t * lax.rsqrt(ms + EPS) * gain


def _mixin_body(seq, tl, x_ref, g_ref, w_ref, qn_ref, kn_ref, bd_ref,
                z_ref, xbc_ref, dt_ref, k_ref, v_ref, qa_ref, ka_ref, va_ref):
    j = pl.program_id(1)
    h = _rms(x_ref[0], g_ref[...]).astype(BF16)
    p = _dot(h, w_ref[...])
    z_ref[0] = p[:, _C_Z:_C_XBC]
    xbc_ref[0] = p[:, _C_XBC:_C_Q]
    dt_ref[0] = p[:, _C_DT:_C_END]
    bd = bd_ref[...]
    q = _head_norm(p[:, _C_Q:_C_K], qn_ref[...], bd)
    k = _head_norm(p[:, _C_K:_C_V], kn_ref[...], bd)
    v = p[:, _C_V:_C_DT]
    k_ref[0] = k
    v_ref[0] = v
    valid = (j * tl + _iota((tl, 1), 0)) < seq
    qa_ref[0] = jnp.where(valid, q * (HEAD_DIM ** -0.5), 0.0).astype(BF16)
    ka_ref[0] = jnp.where(valid, k, 0.0).astype(BF16)
    va_ref[0] = jnp.where(valid, v, 0.0).astype(BF16)


def _mixin(x, layer, norm, w_in, qn, kn, ones_bd, seq_pad):
    nb, seq, d = x.shape
    tl = _pick_tile(seq_pad, 640, 16)
    blk = lambda b, j: (b, j, 0)
    lay3 = lambda b, j: (layer, 0, 0)
    f32_out = lambda c: jax.ShapeDtypeStruct((nb, seq, c), F32)
    att_out = jax.ShapeDtypeStruct((nb, seq_pad, WIDTH), BF16)
    return pl.pallas_call(
        functools.partial(_mixin_body, seq, tl),
        grid=(nb, seq_pad // tl),
        in_specs=[pl.BlockSpec((1, tl, d), blk), _resident((None, 1, d), lay3),
                  _resident((None, d, _C_END), lay3), _resident((None, 1, WIDTH), lay3),
                  _resident((None, 1, WIDTH), lay3), _resident((WIDTH, WIDTH), lambda b, j: (0, 0))],
        out_specs=[pl.BlockSpec((1, tl, WIDTH), blk), pl.BlockSpec((1, tl, CONV_DIM), blk),
                   pl.BlockSpec((1, tl, DT_PAD), blk), pl.BlockSpec((1, tl, WIDTH), blk),
                   pl.BlockSpec((1, tl, WIDTH), blk), pl.BlockSpec((1, tl, WIDTH), blk),
                   pl.BlockSpec((1, tl, WIDTH), blk), pl.BlockSpec((1, tl, WIDTH), blk)],
        out_shape=[f32_out(WIDTH), f32_out(CONV_DIM), f32_out(DT_PAD), f32_out(WIDTH), f32_out(WIDTH),
                   att_out, att_out, att_out],
        compiler_params=pltpu.CompilerParams(dimension_semantics=("parallel", "parallel"),
                                             vmem_limit_bytes=VMEM_LIMIT),
        name="mix_in",
    )(x, norm, w_in, qn, kn, ones_bd)


def _ssd_body(seq, q, xbc_ref, z_ref, dt_ref, cinit_ref, sinit_ref, cw_ref, cb_ref, dtb_ref, alog_ref,
              dsk_ref, nw_ref, exp_ref, y_ref, sout_ref, cbuf, st):
    c = pl.program_id(1)

    @pl.when(c == 0)
    def _():
        cbuf[0:8, :] = cinit_ref[0]
        st[...] = sinit_ref[0]

    valid = (c * q + _iota((q, 1), 0)) < seq
    xraw = jnp.where(valid, xbc_ref[0], 0.0)
    cbuf[8:8 + q, :] = xraw
    cw = cw_ref[...]
    conv = (cbuf[5:5 + q, :] * cw[0:1] + cbuf[6:6 + q, :] * cw[1:2]
            + cbuf[7:7 + q, :] * cw[2:3] + xraw * cw[3:4])
    cbuf[5:8, :] = cbuf[5 + q:8 + q, :]
    xc = _silu(conv + cb_ref[...])
    xs = xc[:, :WIDTH]
    bm = xc[:, WIDTH:WIDTH + LANES].astype(BF16)
    cm = xc[:, WIDTH + LANES:].astype(BF16)

    dt = jnp.where(valid, _softplus(dt_ref[0] + dtb_ref[...]), 0.0)
    a = dt * (-jnp.exp(alog_ref[...]))
    tri = jnp.where(_iota((q, q), 0) >= _iota((q, q), 1), 1.0, 0.0).astype(BF16)
    cs = None
    for part in _split(a, 3):
        term = _dot(tri, part)
        cs = term if cs is None else cs + term
    ecs = jnp.exp(cs)
    to_end = jnp.exp(cs[q - 1:q, :] - cs)
    expand = exp_ref[...]
    dt_e = _split_dot(dt, expand, 2)
    ecs_e = _split_dot(ecs, expand, 2)
    to_end_e = _split_dot(to_end, expand, 2)
    xdt = xs * dt_e

    lane = _iota((q, LANES), 1)
    low = lane < HEAD_DIM
    g0 = _dot_nt(jnp.where(low, cm, jnp.zeros_like(cm)), bm)
    g1 = _dot_nt(jnp.where(low, jnp.zeros_like(cm), cm), bm)
    cs_t = cs.T
    causal = _iota((q, q), 0) >= _iota((q, q), 1)
    xdt_b = xdt.astype(BF16)
    y_pairs = []
    for p in range(N_PAIRS):
        xp = xdt_b[:, p * LANES:(p + 1) * LANES]
        yp = None
        for h, xm in ((2 * p, jnp.where(low, xp, jnp.zeros_like(xp))),
                      (2 * p + 1, jnp.where(low, jnp.zeros_like(xp), xp))):
            seg = cs[:, h:h + 1] - cs_t[h:h + 1, :]
            decay = jnp.exp(jnp.where(causal, seg, -jnp.inf))
            m = ((g0 if h < N_HEADS // SSD_GROUPS else g1) * decay).astype(BF16)
            t = _dot(m, xm)
            yp = t if yp is None else yp + t
        y_pairs.append(yp)
    y_diag = jnp.concatenate(y_pairs, axis=1)

    s_prev = st[...]
    y_off = _dot(cm, s_prev.astype(BF16)) * ecs_e
    y = y_diag + y_off + dsk_ref[...] * xs
    zz = jnp.where(valid, z_ref[0], 0.0)
    y = y * _silu(zz)
    half = WIDTH // SSD_GROUPS
    y0, y1 = y[:, :half], y[:, half:]
    y0 = y0 * lax.rsqrt(jnp.mean(y0 * y0, axis=-1, keepdims=True) + EPS)
    y1 = y1 * lax.rsqrt(jnp.mean(y1 * y1, axis=-1, keepdims=True) + EPS)
    y_ref[0] = (jnp.concatenate([y0, y1], axis=1) * nw_ref[...]).astype(y_ref.dtype)

    upd = _dot_tn(bm, (xdt * to_end_e).astype(BF16))
    own = (_iota((LANES, WIDTH), 0) < HEAD_DIM) == (_iota((LANES, WIDTH), 1) < half)
    st[...] = s_prev * ecs_e[q - 1:q, :] + jnp.where(own, upd, 0.0)

    @pl.when(c == pl.num_programs(1) - 1)
    def _():
        sout_ref[0] = st[...]


def _ssd(xbc, z, dt, seq, conv_init, state_init, layer, cw, cb, dtb, alog, dsk, nw, expand):
    nb, n_rows, _ = xbc.shape
    q = CHUNK
    nchunk = pl.cdiv(n_rows, q)
    blk = lambda b, c: (b, c, 0)
    per_b = lambda b, c: (b, 0, 0)
    lay3 = lambda b, c: (layer, 0, 0)
    return pl.pallas_call(
        functools.partial(_ssd_body, seq, q),
        grid=(nb, nchunk),
        in_specs=[pl.BlockSpec((1, q, CONV_DIM), blk), pl.BlockSpec((1, q, WIDTH), blk),
                  pl.BlockSpec((1, q, DT_PAD), blk), pl.BlockSpec((1, 8, CONV_DIM), per_b),
                  pl.BlockSpec((1, LANES, WIDTH), per_b), pl.BlockSpec((None, 8, CONV_DIM), lay3),
                  pl.BlockSpec((None, 1, CONV_DIM), lay3), pl.BlockSpec((None, 1, DT_PAD), lay3),
                  pl.BlockSpec((None, 1, DT_PAD), lay3), pl.BlockSpec((None, 1, WIDTH), lay3),
                  pl.BlockSpec((None, 1, WIDTH), lay3), pl.BlockSpec((LANES, WIDTH), lambda b, c: (0, 0))],
        out_specs=[pl.BlockSpec((1, q, WIDTH), blk), pl.BlockSpec((1, LANES, WIDTH), per_b)],
        out_shape=[jax.ShapeDtypeStruct((nb, n_rows, WIDTH), BF16),
                   jax.ShapeDtypeStruct((nb, LANES, WIDTH), F32)],
        scratch_shapes=[pltpu.VMEM((8 + q, CONV_DIM), F32), pltpu.VMEM((LANES, WIDTH), F32)],
        compiler_params=pltpu.CompilerParams(dimension_semantics=("parallel", "arbitrary"),
                                             vmem_limit_bytes=VMEM_LIMIT),
        name="ssd",
    )(xbc, z, dt, conv_init, state_init, cw, cb, dtb, alog, dsk, nw, expand)


def _state_to_blocks(s):
    nb = s.shape[0]
    hpg = N_HEADS // SSD_GROUPS
    t = jnp.transpose(s.reshape(nb, SSD_GROUPS, hpg, HEAD_DIM, HEAD_DIM), (0, 1, 4, 2, 3))
    t = t.reshape(nb, SSD_GROUPS, HEAD_DIM, hpg * HEAD_DIM)
    zero = jnp.zeros_like(t[:, 0])
    top = jnp.concatenate([t[:, 0], zero], axis=-1)
    bot = jnp.concatenate([zero, t[:, 1]], axis=-1)
    return jnp.concatenate([top, bot], axis=1)


def _blocks_to_state(sb):
    nb = sb.shape[0]
    hpg = N_HEADS // SSD_GROUPS
    half = WIDTH // SSD_GROUPS
    t = jnp.stack([sb[:, :HEAD_DIM, :half], sb[:, HEAD_DIM:, half:]], axis=1)
    t = t.reshape(nb, SSD_GROUPS, HEAD_DIM, hpg, HEAD_DIM)
    return jnp.transpose(t, (0, 1, 3, 4, 2)).reshape(nb, N_HEADS, HEAD_DIM, HEAD_DIM)


def _sb_softplus(z):
    return jnp.maximum(z, 0.0) + jnp.log(1.0 + jnp.exp2(jnp.abs(z) * (-LOG2E)))


def _suffix_matrix(tk):
    return jnp.where(_iota((tk, tk), 0) >= _iota((tk, tk), 1), 1.0, 0.0).astype(BF16)


def _attn_body(q_ref, k_ref, v_ref, bias_ref, g_ref, o_ref, qm_sc, z_sc, r_sc, acc_sc):
    i = pl.program_id(1)
    tq, tk = CHUNK, KEY_TILE
    low = _iota((tq, LANES), 1) < HEAD_DIM
    for p in range(N_PAIRS):
        qp = q_ref[0, :, p * LANES:(p + 1) * LANES]
        qm_sc[2 * p] = jnp.where(low, qp, jnp.zeros_like(qp))
        qm_sc[2 * p + 1] = jnp.where(low, jnp.zeros_like(qp), qp)
    u = _suffix_matrix(tk)
    u2 = jnp.concatenate([u, u], axis=0)
    kt_diag = i // 2
    visible = _iota((tq, tk), 1) < _iota((tq, tk), 0) + (i - 2 * kt_diag) * tq
    heads = range(N_HEADS)
    pair = lambda h: slice((h // 2) * LANES, (h // 2 + 1) * LANES)

    def scores(kt):
        ks = pl.multiple_of(kt * tk, tk)
        return [_dot_nt(qm_sc[h], k_ref[0, pl.ds(ks, tk), pair(h)]) + bias_ref[h] for h in heads]

    def tile(kt, zs, diagonal, next_slot):
        ks = pl.multiple_of(kt * tk, tk)
        ps = [_sb_softplus(z) for z in zs]
        if diagonal:
            ps = [jnp.where(visible, p, 0.0) for p in ps]
        sufs = [_dot(jnp.concatenate(_split(p, 2), axis=1), u2) for p in ps]
        ws = [jnp.exp(z - s) for z, s in zip(zs, sufs)]
        if diagonal:
            ws = [jnp.where(visible, w, 0.0) for w in ws]
        pvs = [_dot(ws[h].astype(BF16), v_ref[0, pl.ds(ks, tk), pair(h)]) for h in heads]
        for h, z in zip(heads, scores(jnp.maximum(kt - 1, 0))):
            z_sc[next_slot, h] = z
        for h in heads:
            total = jnp.sum(ps[h], axis=1, keepdims=True)
            if diagonal:
                acc_sc[h] = pvs[h]
                r_sc[h] = jnp.broadcast_to(total, (tq, LANES))
            else:
                later = r_sc[h]
                acc_sc[h] = acc_sc[h] + jnp.exp(-later) * pvs[h]
                r_sc[h] = later + total

    tile(kt_diag, scores(kt_diag), True, 0)

    def body(j, carry):
        slot = j % 2
        tile(kt_diag - 1 - j, [z_sc[slot, h] for h in heads], False, 1 - slot)
        return carry

    lax.fori_loop(0, kt_diag, body, 0)
    outs = [jnp.where(low, acc_sc[2 * p], acc_sc[2 * p + 1]) for p in range(N_PAIRS)]
    o_ref[0] = _rms(jnp.concatenate(outs, axis=1), g_ref[...]).astype(o_ref.dtype)


def _attn(qa, ka, va, seq, layer, bias, out_norm):
    nb, seq_pad, _ = qa.shape
    assert seq_pad % KEY_TILE == 0
    nq = pl.cdiv(seq, CHUNK)
    head_tile = lambda: pltpu.VMEM((N_HEADS, CHUNK, LANES), F32)
    return pl.pallas_call(
        _attn_body,
        grid=(nb, nq),
        in_specs=[pl.BlockSpec((1, CHUNK, WIDTH), lambda b, i: (b, i, 0)),
                  pl.BlockSpec((1, seq_pad, WIDTH), lambda b, i: (b, 0, 0)),
                  pl.BlockSpec((1, seq_pad, WIDTH), lambda b, i: (b, 0, 0)),
                  pl.BlockSpec(memory_space=pltpu.SMEM),
                  pl.BlockSpec((None, 1, WIDTH), lambda b, i: (layer, 0, 0))],
        out_specs=pl.BlockSpec((1, CHUNK, WIDTH), lambda b, i: (b, i, 0)),
        out_shape=jax.ShapeDtypeStruct((nb, seq, WIDTH), BF16),
        scratch_shapes=[pltpu.VMEM((N_HEADS, CHUNK, LANES), BF16),
                        pltpu.VMEM((2, N_HEADS, CHUNK, KEY_TILE), F32), head_tile(), head_tile()],
        compiler_params=pltpu.CompilerParams(dimension_semantics=("parallel", "arbitrary"),
                                             vmem_limit_bytes=VMEM_LIMIT),
        name="sb_attn",
    )(qa, ka, va, bias, out_norm)


def _decode_body(n_group, pt_ref, q_ref, kn_ref, vn_ref, bias_ref, g_ref, *rest):
    k_refs = rest[:n_group]
    v_refs = rest[n_group:2 * n_group]
    o_ref, r_sc, acc_sc = rest[2 * n_group:]
    j = pl.program_id(1)
    t_new = q_ref.shape[1]
    rows = N_HEADS * t_new
    page = CHUNK
    u = _suffix_matrix(page)
    t_bits, d_bits = t_new.bit_length() - 1, HEAD_DIM.bit_length() - 1
    own = (_iota((rows, WIDTH), 0) >> t_bits) == (_iota((rows, WIDTH), 1) >> d_bits)
    q = q_ref[0]
    qbd = jnp.where(own, jnp.concatenate([q] * N_HEADS, axis=0), 0.0).astype(BF16)
    bias = bias_ref[...]

    @pl.when(j == 0)
    def _():
        pad = jnp.zeros((page - t_new, WIDTH), F32)
        kb = jnp.concatenate([kn_ref[0], pad], axis=0).astype(BF16)
        vb = jnp.concatenate([vn_ref[0], pad], axis=0).astype(BF16)
        z = _dot_nt(qbd, kb) + bias
        visible = _iota((rows, page), 1) < (_iota((rows, page), 0) & (t_new - 1))
        p = jnp.where(visible, _sb_softplus(z), 0.0)
        w = jnp.where(visible, jnp.exp(z - _split_dot(p, u, 2)), 0.0)
        acc_sc[...] = _dot(w.astype(BF16), vb)
        r_sc[...] = jnp.broadcast_to(jnp.sum(p, axis=1, keepdims=True), (rows, LANES))

    pages = range(n_group)
    as_matrix = lambda ref: ref[...].reshape(page, WIDTH).astype(BF16)
    zs = [_dot_nt(qbd, as_matrix(k_refs[g])) + bias for g in pages]
    ps = [_sb_softplus(z) for z in zs]
    u2 = jnp.concatenate([u, u], axis=0)
    sufs = [_dot(jnp.concatenate(_split(p, 2), axis=1), u2) for p in ps]
    later = r_sc[...]
    ws = []
    for g in pages:
        ws.append(jnp.exp(zs[g] - later - sufs[g]).astype(BF16))
        later = later + jnp.sum(ps[g], axis=1, keepdims=True)
    acc = acc_sc[...]
    for g in pages:
        acc = acc + _dot(ws[g], as_matrix(v_refs[g]))
    acc_sc[...] = acc
    r_sc[...] = later

    @pl.when(j == pl.num_programs(1) - 1)
    def _():
        a = jnp.where(own, acc, 0.0)
        o = a[0:t_new]
        for h in range(1, N_HEADS):
            o = o + a[h * t_new:(h + 1) * t_new]
        o_ref[0] = _rms(o, g_ref[...]).astype(o_ref.dtype)


def _decode(q, k_new, v_new, cache_k, cache_v, page_table, layer, bias_tile, out_norm):
    db, t_new, _ = q.shape
    n_pages = page_table.shape[1]
    page = cache_k.shape[2]
    assert page == CHUNK and t_new == 8
    n_group = _pick_tile(n_pages, 8, 1)
    rows = N_HEADS * t_new
    per_b = lambda b, j, pt: (b, 0, 0)

    def page_spec(g):
        return pl.BlockSpec((None, None, page, N_HEADS, HEAD_DIM),
                            lambda b, j, pt: (layer, pt[b, n_pages - 1 - (j * n_group + g)], 0, 0, 0))

    grid_spec = pltpu.PrefetchScalarGridSpec(
        num_scalar_prefetch=1,
        grid=(db, n_pages // n_group),
        in_specs=[pl.BlockSpec((1, t_new, WIDTH), per_b), pl.BlockSpec((1, t_new, WIDTH), per_b),
                  pl.BlockSpec((1, t_new, WIDTH), per_b),
                  pl.BlockSpec((None, rows, LANES), lambda b, j, pt: (layer, 0, 0)),
                  pl.BlockSpec((None, 1, WIDTH), lambda b, j, pt: (layer, 0, 0))]
                 + [page_spec(g) for g in range(n_group)] * 2,
        out_specs=pl.BlockSpec((1, t_new, WIDTH), per_b),
        scratch_shapes=[pltpu.VMEM((rows, LANES), F32), pltpu.VMEM((rows, WIDTH), F32)],
    )
    return pl.pallas_call(
        functools.partial(_decode_body, n_group),
        grid_spec=grid_spec,
        out_shape=jax.ShapeDtypeStruct((db, t_new, WIDTH), F32),
        compiler_params=pltpu.CompilerParams(dimension_semantics=("parallel", "arbitrary"),
                                             vmem_limit_bytes=VMEM_LIMIT),
        name="sb_decode",
    )(page_table, q, k_new, v_new, bias_tile, out_norm, *([cache_k] * n_group), *([cache_v] * n_group))


def kernel(x_prompt, x_sample, cache_k, cache_v, state_ssm, state_conv, page_table, meta_tokens,
           norm_ffn1, ffn1_w_gu, ffn1_w_down, norm_mix, w_in, conv_w, conv_b, dt_bias, A_log,
           D_skip, ssd_norm, q_norm, k_norm, sb_bias, sb_out_norm, w_out, norm_ffn2, ffn2_w_gu,
           ffn2_w_down):
    bp, seq_in, d = x_prompt.shape
    db, t_new, _ = x_sample.shape
    depth = w_in.shape[0]
    n_meta = meta_tokens.shape[0]
    seq = n_meta + seq_in
    seq_pad = pl.cdiv(seq, KEY_TILE) * KEY_TILE

    wgu1, wd1 = ffn1_w_gu.astype(BF16), ffn1_w_down.astype(BF16)
    wgu2, wd2 = ffn2_w_gu.astype(BF16), ffn2_w_down.astype(BF16)
    wo = w_out.astype(BF16)
    o_xbc, o_dt = WIDTH + CONV_DIM, WIDTH + CONV_DIM + N_HEADS
    w_dt = jnp.pad(w_in[:, :, o_xbc:o_dt], ((0, 0), (0, 0), (0, DT_PAD - N_HEADS)))
    w_in_r = jnp.concatenate([w_in[:, :, :o_xbc], w_in[:, :, o_dt:], w_dt], axis=-1).astype(BF16)
    assert w_in_r.shape[-1] == _C_END
    vec = lambda a: a[:, None, :]
    norm_ffn1, norm_mix, norm_ffn2 = vec(norm_ffn1), vec(norm_mix), vec(norm_ffn2)
    qn = vec(jnp.tile(q_norm, (1, N_HEADS)))
    kn = vec(jnp.tile(k_norm, (1, N_HEADS)))
    ones_bd = ((jnp.arange(WIDTH)[:, None] // HEAD_DIM) == (jnp.arange(WIDTH)[None, :] // HEAD_DIM)).astype(BF16)
    expand = (jnp.arange(LANES)[:, None] == (jnp.arange(WIDTH)[None, :] // HEAD_DIM)).astype(BF16)
    cw = jnp.pad(conv_w, ((0, 0), (0, 8 - CONV_W), (0, 0)))
    conv_b = vec(conv_b)
    dtb = vec(jnp.pad(dt_bias, ((0, 0), (0, DT_PAD - N_HEADS))))
    alog = vec(jnp.pad(A_log, ((0, 0), (0, DT_PAD - N_HEADS))))
    dsk = vec(jnp.repeat(D_skip, HEAD_DIM, axis=1))
    ssd_norm, sb_out_norm = vec(ssd_norm), vec(sb_out_norm)
    bias_tile = jnp.broadcast_to(jnp.repeat(sb_bias, t_new, axis=1)[:, :, None], (depth, N_HEADS * t_new, LANES))

    meta = jnp.broadcast_to(meta_tokens[None].astype(x_prompt.dtype), (bp, n_meta, d))
    xp = jnp.concatenate([meta, x_prompt], axis=1).reshape(bp * seq, d)
    xs = x_sample.reshape(db * t_new, d)
    conv0_p = jnp.zeros((bp, 8, CONV_DIM), F32)
    state0_p = jnp.zeros((bp, LANES, WIDTH), F32)

    outs = {name: [] for name in ("kp", "vp", "sp", "cp", "ks", "vs", "ss", "cs")}
    for l in range(depth):
        xp = _ffn(xp, l, norm_ffn1, wgu1, wd1)
        z, xbc, dt, k, v, qa, ka, va = _mixin(xp.reshape(bp, seq, d), l, norm_mix, w_in_r, qn, kn, ones_bd, seq_pad)
        y_ssd, s_fin = _ssd(xbc, z, dt, seq, conv0_p, state0_p, l, cw, conv_b, dtb, alog, dsk, ssd_norm, expand)
        y_sb = _attn(qa, ka, va, seq, l, sb_bias[l], sb_out_norm)
        xp = _ffn(xp, l, norm_ffn2, wgu2, wd2,
                  mix=(y_ssd.reshape(bp * seq, WIDTH), y_sb.reshape(bp * seq, WIDTH), wo))
        outs["kp"].append(k.reshape(bp, seq, N_HEADS, HEAD_DIM))
        outs["vp"].append(v.reshape(bp, seq, N_HEADS, HEAD_DIM))
        outs["sp"].append(_blocks_to_state(s_fin))
        outs["cp"].append(xbc[:, seq - (CONV_W - 1):, :])

        xs = _ffn(xs, l, norm_ffn1, wgu1, wd1)
        z, xbc, dt, k, v, qa, _, _ = _mixin(xs.reshape(1, db * t_new, d), l, norm_mix, w_in_r, qn, kn, ones_bd,
                                            db * t_new)
        shp = lambda a: a.reshape(db, t_new, a.shape[-1])
        xbc, k, v = shp(xbc), shp(k), shp(v)
        chunk = lambda a: jnp.pad(shp(a), ((0, 0), (0, CHUNK - t_new), (0, 0)))
        conv0_s = jnp.pad(state_conv[l], ((0, 0), (8 - (CONV_W - 1), 0), (0, 0)))
        y_ssd, s_fin = _ssd(chunk(xbc), chunk(z), chunk(dt), t_new, conv0_s, _state_to_blocks(state_ssm[l]), l,
                            cw, conv_b, dtb, alog, dsk, ssd_norm, expand)
        y_ssd = y_ssd[:, :t_new]
        y_sb = _decode(shp(qa).astype(F32), k, v, cache_k, cache_v, page_table, l, bias_tile, sb_out_norm)
        xs = _ffn(xs, l, norm_ffn2, wgu2, wd2,
                  mix=(y_ssd.reshape(db * t_new, WIDTH), y_sb.reshape(db * t_new, WIDTH), wo))
        outs["ks"].append(k.reshape(db, t_new, N_HEADS, HEAD_DIM))
        outs["vs"].append(v.reshape(db, t_new, N_HEADS, HEAD_DIM))
        outs["ss"].append(_blocks_to_state(s_fin))
        outs["cs"].append(jnp.concatenate([state_conv[l], xbc], axis=1)[:, -(CONV_W - 1):, :])

    y_prompt = xp.reshape(bp, seq, d)[:, n_meta:]
    y_sample = xs.reshape(db, t_new, d)
    st = lambda name: jnp.stack(outs[name])
    return (y_prompt, y_sample, st("kp"), st("vp"), st("sp"), st("cp"),
            st("ks"), st("vs"), st("ss"), st("cs"))
```

```python
import functools

import jax
import jax.numpy as jnp
from jax import lax
from jax.experimental import pallas as pl
from jax.experimental.pallas import tpu as pltpu

F32 = jnp.float32
BF16 = jnp.bfloat16
EPS = 1e-6
LOG2E = 1.4426950408889634

LANES = 128
HEAD_DIM = 64
N_HEADS = 8
WIDTH = N_HEADS * HEAD_DIM
N_PAIRS = WIDTH // LANES
SSD_GROUPS = 2
CONV_W = 4
CONV_DIM = WIDTH + 2 * SSD_GROUPS * HEAD_DIM
DT_PAD = LANES
CHUNK = 128
KEY_TILE = 256
VMEM_LIMIT = 56 * 1024 * 1024


def _dot(a, b):
    return jnp.dot(a, b, preferred_element_type=F32)


def _dot_nt(a, b):
    return lax.dot_general(a, b, (((1,), (1,)), ((), ())), preferred_element_type=F32)


def _dot_tn(a, b):
    return lax.dot_general(a, b, (((0,), (0,)), ((), ())), preferred_element_type=F32)


def _split(a, passes):
    parts = []
    rem = a
    for p in range(passes):
        part = rem.astype(BF16)
        parts.append(part)
        if p + 1 < passes:
            rem = rem - part.astype(F32)
    return parts


def _split_dot(a, m, passes):
    out = None
    for part in _split(a, passes):
        term = _dot(part, m)
        out = term if out is None else out + term
    return out


def _rms(x, g):
    ms = jnp.mean(x * x, axis=-1, keepdims=True)
    return x * lax.rsqrt(ms + EPS) * g


def _silu(x):
    return x * jax.nn.sigmoid(x)


def _softplus(x):
    return jnp.maximum(x, 0.0) + jnp.log1p(jnp.exp(-jnp.abs(x)))


def _iota(shape, dim):
    return lax.broadcasted_iota(jnp.int32, shape, dim)


def _pick_tile(n, cap, mult):
    if n <= cap:
        return n
    best = None
    for t in range(mult, cap + 1, mult):
        if n % t == 0:
            best = t
    assert best is not None, (n, cap, mult)
    return best


def _resident(shape, index_map):
    return pl.BlockSpec(shape, index_map, pipeline_mode=pl.Buffered(1))


def _ffn_body(has_mix, n_chunks, fc, *refs):
    if has_mix:
        x_ref, ys_ref, yb_ref, wo_ref, g_ref, wgu_ref, wd_ref, o_ref = refs
        x = (x_ref[...]
             + _dot(ys_ref[...].astype(BF16), wo_ref[:WIDTH, :])
             + _dot(yb_ref[...].astype(BF16), wo_ref[WIDTH:, :]))
    else:
        x_ref, g_ref, wgu_ref, wd_ref, o_ref = refs
        x = x_ref[...]
    h = _rms(x, g_ref[...]).astype(BF16)
    d_ff = n_chunks * fc
    acc = None
    for c in range(n_chunks):
        gate = _dot(h, wgu_ref[:, c * fc:(c + 1) * fc])
        up = _dot(h, wgu_ref[:, d_ff + c * fc:d_ff + (c + 1) * fc])
        a = (_silu(gate) * up).astype(BF16)
        d = _dot(a, wd_ref[c * fc:(c + 1) * fc, :])
        acc = d if acc is None else acc + d
    o_ref[...] = x + 0.5 * acc


def _ffn(x, layer, norm, wgu, wd, mix=None):
    t, d = x.shape
    d_ff = wd.shape[1]
    fc = 256
    assert d_ff % fc == 0
    tm = _pick_tile(t, 704, 16)
    row = lambda i: (i, 0)
    lay3 = lambda i: (layer, 0, 0)
    in_specs = [pl.BlockSpec((tm, d), row)]
    args = [x]
    if mix is not None:
        ys, yb, wo = mix
        in_specs += [pl.BlockSpec((tm, WIDTH), row), pl.BlockSpec((tm, WIDTH), row),
                     _resident((None, 2 * WIDTH, d), lay3)]
        args += [ys, yb, wo]
    in_specs += [_resident((None, 1, d), lay3), _resident((None, d, 2 * d_ff), lay3),
                 _resident((None, d_ff, d), lay3)]
    args += [norm, wgu, wd]
    return pl.pallas_call(
        functools.partial(_ffn_body, mix is not None, d_ff // fc, fc),
        grid=(t // tm,),
        in_specs=in_specs,
        out_specs=pl.BlockSpec((tm, d), row),
        out_shape=jax.ShapeDtypeStruct((t, d), F32),
        compiler_params=pltpu.CompilerParams(dimension_semantics=("parallel",), vmem_limit_bytes=VMEM_LIMIT),
        name="ffn_mix" if mix is not None else "ffn",
    )(*args)


_C_Z, _C_XBC, _C_Q, _C_K, _C_V, _C_DT, _C_END = 0, 512, 1280, 1792, 2304, 2816, 2944


def _head_norm(t, gain, ones_bd):
    ms = _split_dot(t * t, ones_bd, 2) * (1.0 / HEAD_DIM)
    return t * lax.rsqrt(ms + EPS) * gain


def _mixin_body(seq, tl, kv_transposed, x_ref, g_ref, w_ref, qn_ref, kn_ref, bd_ref, *refs):
    z_ref, xbc_ref, dt_ref, k_ref, v_ref, qa_ref, ka_ref, va_ref = refs[-8:]
    j = pl.program_id(1)
    h = _rms(x_ref[0], g_ref[...]).astype(BF16)
    p = _dot(h, w_ref[...])
    z_ref[0] = p[:, _C_Z:_C_XBC]
    xbc_ref[0] = p[:, _C_XBC:_C_Q]
    dt_ref[0] = p[:, _C_DT:_C_END]
    bd = bd_ref[...]
    q = _head_norm(p[:, _C_Q:_C_K], qn_ref[...], bd)
    k = _head_norm(p[:, _C_K:_C_V], kn_ref[...], bd)
    v = p[:, _C_V:_C_DT]
    if kv_transposed:
        k_t, v_t = k.T, v.T
        for s in range(k_ref.shape[0]):
            k_ref[s, 0] = k_t
            v_ref[s, 0] = v_t
    else:
        k_ref[0] = k
        v_ref[0] = v
    valid = (j * tl + _iota((tl, 1), 0)) < seq
    qa_ref[0] = jnp.where(valid, q * (HEAD_DIM ** -0.5), 0.0).astype(BF16)
    ka_ref[0] = jnp.where(valid, k, 0.0).astype(BF16)
    va_ref[0] = jnp.where(valid, v, 0.0).astype(BF16)


def _mixin(x, layer, norm, w_in, qn, kn, ones_bd, seq_pad, kv_stack=None):
    nb, seq, d = x.shape
    tl = _pick_tile(seq_pad, 512, 16 if kv_stack is None else LANES)
    blk = lambda b, j: (b, j, 0)
    lay3 = lambda b, j: (layer, 0, 0)
    f32_out = lambda c: jax.ShapeDtypeStruct((nb, seq, c), F32)
    att_out = jax.ShapeDtypeStruct((nb, seq_pad, WIDTH), BF16)
    args = [x, norm, w_in, qn, kn, ones_bd]
    in_specs = [pl.BlockSpec((1, tl, d), blk), _resident((None, 1, d), lay3),
                _resident((None, d, _C_END), lay3), _resident((None, 1, WIDTH), lay3),
                _resident((None, 1, WIDTH), lay3), _resident((WIDTH, WIDTH), lambda b, j: (0, 0))]
    aliases = {}
    if kv_stack is None:
        kv_spec = pl.BlockSpec((1, tl, WIDTH), blk)
        kv_out = f32_out(WIDTH)
    else:
        depth, k_stack, v_stack = kv_stack
        kv_out = jax.ShapeDtypeStruct((depth, nb, WIDTH, seq), F32)
        if k_stack is None:
            assert layer == 0
            kv_spec = pl.BlockSpec((depth, 1, WIDTH, tl), lambda b, j: (0, b, 0, j))
        else:
            kv_spec = pl.BlockSpec((1, 1, WIDTH, tl), lambda b, j: (layer, b, 0, j))
            aliases = {len(args): 3, len(args) + 1: 4}
            args += [k_stack, v_stack]
            in_specs += [pl.BlockSpec(memory_space=pl.ANY)] * 2
    return pl.pallas_call(
        functools.partial(_mixin_body, seq, tl, kv_stack is not None),
        grid=(nb, seq_pad // tl),
        in_specs=in_specs,
        out_specs=[pl.BlockSpec((1, tl, WIDTH), blk), pl.BlockSpec((1, tl, CONV_DIM), blk),
                   pl.BlockSpec((1, tl, DT_PAD), blk), kv_spec, kv_spec, pl.BlockSpec((1, tl, WIDTH), blk),
                   pl.BlockSpec((1, tl, WIDTH), blk), pl.BlockSpec((1, tl, WIDTH), blk)],
        out_shape=[f32_out(WIDTH), f32_out(CONV_DIM), f32_out(DT_PAD), kv_out, kv_out,
                   att_out, att_out, att_out],
        input_output_aliases=aliases,
        compiler_params=pltpu.CompilerParams(dimension_semantics=("parallel", "parallel"),
                                             vmem_limit_bytes=VMEM_LIMIT),
        name="mix_in",
    )(*args)


def _ssd_body(seq, q, xbc_ref, z_ref, dt_ref, cinit_ref, sinit_ref, cw_ref, cb_ref, dtb_ref, alog_ref,
              dsk_ref, nw_ref, exp_ref, y_ref, sout_ref, cbuf, st):
    c = pl.program_id(1)

    @pl.when(c == 0)
    def _():
        cbuf[0:8, :] = cinit_ref[0]
        st[...] = sinit_ref[0]

    valid = (c * q + _iota((q, 1), 0)) < seq
    xraw = jnp.where(valid, xbc_ref[0], 0.0)
    cbuf[8:8 + q, :] = xraw
    cw = cw_ref[...]
    conv = (cbuf[5:5 + q, :] * cw[0:1] + cbuf[6:6 + q, :] * cw[1:2]
            + cbuf[7:7 + q, :] * cw[2:3] + xraw * cw[3:4])
    cbuf[5:8, :] = cbuf[5 + q:8 + q, :]
    xc = _silu(conv + cb_ref[...])
    xs = xc[:, :WIDTH]
    bm = xc[:, WIDTH:WIDTH + LANES].astype(BF16)
    cm = xc[:, WIDTH + LANES:].astype(BF16)

    dt = jnp.where(valid, _softplus(dt_ref[0] + dtb_ref[...]), 0.0)
    a = dt * (-jnp.exp(alog_ref[...]))
    tri = jnp.where(_iota((q, q), 0) >= _iota((q, q), 1), 1.0, 0.0).astype(BF16)
    cs = None
    for part in _split(a, 3):
        term = _dot(tri, part)
        cs = term if cs is None else cs + term
    ecs = jnp.exp(cs)
    to_end = jnp.exp(cs[q - 1:q, :] - cs)
    expand = exp_ref[...]
    dt_e = _split_dot(dt, expand, 2)
    ecs_e = _split_dot(ecs, expand, 2)
    to_end_e = _split_dot(to_end, expand, 2)
    xdt = xs * dt_e

    lane = _iota((q, LANES), 1)
    low = lane < HEAD_DIM
    g0 = _dot_nt(jnp.where(low, cm, jnp.zeros_like(cm)), bm)
    g1 = _dot_nt(jnp.where(low, jnp.zeros_like(cm), cm), bm)
    cs_t = cs.T
    causal = _iota((q, q), 0) >= _iota((q, q), 1)
    xdt_b = xdt.astype(BF16)
    y_pairs = []
    for p in range(N_PAIRS):
        xp = xdt_b[:, p * LANES:(p + 1) * LANES]
        yp = None
        for h, xm in ((2 * p, jnp.where(low, xp, jnp.zeros_like(xp))),
                      (2 * p + 1, jnp.where(low, jnp.zeros_like(xp), xp))):
            seg = cs[:, h:h + 1] - cs_t[h:h + 1, :]
            decay = jnp.exp(jnp.where(causal, seg, -jnp.inf))
            m = ((g0 if h < N_HEADS // SSD_GROUPS else g1) * decay).astype(BF16)
            t = _dot(m, xm)
            yp = t if yp is None else yp + t
        y_pairs.append(yp)
    y_diag = jnp.concatenate(y_pairs, axis=1)

    s_prev = st[...]
    y_off = _dot(cm, s_prev.astype(BF16)) * ecs_e
    y = y_diag + y_off + dsk_ref[...] * xs
    zz = jnp.where(valid, z_ref[0], 0.0)
    y = y * _silu(zz)
    half = WIDTH // SSD_GROUPS
    y0, y1 = y[:, :half], y[:, half:]
    y0 = y0 * lax.rsqrt(jnp.mean(y0 * y0, axis=-1, keepdims=True) + EPS)
    y1 = y1 * lax.rsqrt(jnp.mean(y1 * y1, axis=-1, keepdims=True) + EPS)
    y_ref[0] = (jnp.concatenate([y0, y1], axis=1) * nw_ref[...]).astype(y_ref.dtype)

    upd = _dot_tn(bm, (xdt * to_end_e).astype(BF16))
    own = (_iota((LANES, WIDTH), 0) < HEAD_DIM) == (_iota((LANES, WIDTH), 1) < half)
    st[...] = s_prev * ecs_e[q - 1:q, :] + jnp.where(own, upd, 0.0)

    @pl.when(c == pl.num_programs(1) - 1)
    def _():
        sout_ref[0] = st[...]


def _ssd(xbc, z, dt, seq, conv_init, state_init, layer, cw, cb, dtb, alog, dsk, nw, expand):
    nb, n_rows, _ = xbc.shape
    q = CHUNK
    nchunk = pl.cdiv(n_rows, q)
    blk = lambda b, c: (b, c, 0)
    per_b = lambda b, c: (b, 0, 0)
    lay3 = lambda b, c: (layer, 0, 0)
    return pl.pallas_call(
        functools.partial(_ssd_body, seq, q),
        grid=(nb, nchunk),
        in_specs=[pl.BlockSpec((1, q, CONV_DIM), blk), pl.BlockSpec((1, q, WIDTH), blk),
                  pl.BlockSpec((1, q, DT_PAD), blk), pl.BlockSpec((1, 8, CONV_DIM), per_b),
                  pl.BlockSpec((1, LANES, WIDTH), per_b), pl.BlockSpec((None, 8, CONV_DIM), lay3),
                  pl.BlockSpec((None, 1, CONV_DIM), lay3), pl.BlockSpec((None, 1, DT_PAD), lay3),
                  pl.BlockSpec((None, 1, DT_PAD), lay3), pl.BlockSpec((None, 1, WIDTH), lay3),
                  pl.BlockSpec((None, 1, WIDTH), lay3), pl.BlockSpec((LANES, WIDTH), lambda b, c: (0, 0))],
        out_specs=[pl.BlockSpec((1, q, WIDTH), blk), pl.BlockSpec((1, LANES, WIDTH), per_b)],
        out_shape=[jax.ShapeDtypeStruct((nb, n_rows, WIDTH), BF16),
                   jax.ShapeDtypeStruct((nb, LANES, WIDTH), F32)],
        scratch_shapes=[pltpu.VMEM((8 + q, CONV_DIM), F32), pltpu.VMEM((LANES, WIDTH), F32)],
        compiler_params=pltpu.CompilerParams(dimension_semantics=("parallel", "arbitrary"),
                                             vmem_limit_bytes=VMEM_LIMIT),
        name="ssd",
    )(xbc, z, dt, conv_init, state_init, cw, cb, dtb, alog, dsk, nw, expand)


def _state_to_blocks(s):
    nb = s.shape[0]
    hpg = N_HEADS // SSD_GROUPS
    t = jnp.transpose(s.reshape(nb, SSD_GROUPS, hpg, HEAD_DIM, HEAD_DIM), (0, 1, 4, 2, 3))
    t = t.reshape(nb, SSD_GROUPS, HEAD_DIM, hpg * HEAD_DIM)
    zero = jnp.zeros_like(t[:, 0])
    top = jnp.concatenate([t[:, 0], zero], axis=-1)
    bot = jnp.concatenate([zero, t[:, 1]], axis=-1)
    return jnp.concatenate([top, bot], axis=1)


def _blocks_to_state(sb):
    nb = sb.shape[0]
    hpg = N_HEADS // SSD_GROUPS
    half = WIDTH // SSD_GROUPS
    t = jnp.stack([sb[:, :HEAD_DIM, :half], sb[:, HEAD_DIM:, half:]], axis=1)
    t = t.reshape(nb, SSD_GROUPS, HEAD_DIM, hpg, HEAD_DIM)
    return jnp.transpose(t, (0, 1, 3, 4, 2)).reshape(nb, N_HEADS, HEAD_DIM, HEAD_DIM)


def _sb_softplus(z):
    return jnp.maximum(z, 0.0) + jnp.log(1.0 + jnp.exp2(jnp.abs(z) * (-LOG2E)))


def _suffix_matrix(tk):
    return jnp.where(_iota((tk, tk), 0) >= _iota((tk, tk), 1), 1.0, 0.0).astype(BF16)


def _attn_body(q_ref, k_ref, v_ref, bias_ref, g_ref, o_ref, qm_sc, z_sc, r_sc, acc_sc):
    i = pl.program_id(1)
    tq, tk = CHUNK, KEY_TILE
    low = _iota((tq, LANES), 1) < HEAD_DIM
    for p in range(N_PAIRS):
        qp = q_ref[0, :, p * LANES:(p + 1) * LANES]
        qm_sc[2 * p] = jnp.where(low, qp, jnp.zeros_like(qp))
        qm_sc[2 * p + 1] = jnp.where(low, jnp.zeros_like(qp), qp)
    u = _suffix_matrix(tk)
    u2 = jnp.concatenate([u, u], axis=0)
    kt_diag = i // 2
    visible = _iota((tq, tk), 1) < _iota((tq, tk), 0) + (i - 2 * kt_diag) * tq
    heads = range(N_HEADS)
    pair = lambda h: slice((h // 2) * LANES, (h // 2 + 1) * LANES)

    def scores(kt):
        ks = pl.multiple_of(kt * tk, tk)
        return [_dot_nt(qm_sc[h], k_ref[0, pl.ds(ks, tk), pair(h)]) + bias_ref[h] for h in heads]

    def tile(kt, zs, diagonal, next_slot):
        ks = pl.multiple_of(kt * tk, tk)
        ps = [_sb_softplus(z) for z in zs]
        if diagonal:
            ps = [jnp.where(visible, p, 0.0) for p in ps]
        sufs = [_dot(jnp.concatenate(_split(p, 2), axis=1), u2) for p in ps]
        ws = [jnp.exp(z - s) for z, s in zip(zs, sufs)]
        if diagonal:
            ws = [jnp.where(visible, w, 0.0) for w in ws]
        pvs = [_dot(ws[h].astype(BF16), v_ref[0, pl.ds(ks, tk), pair(h)]) for h in heads]
        for h, z in zip(heads, scores(jnp.maximum(kt - 1, 0))):
            z_sc[next_slot, h] = z
        for h in heads:
            total = jnp.sum(ps[h], axis=1, keepdims=True)
            if diagonal:
                acc_sc[h] = pvs[h]
                r_sc[h] = jnp.broadcast_to(total, (tq, LANES))
            else:
                later = r_sc[h]
                acc_sc[h] = acc_sc[h] + jnp.exp(-later) * pvs[h]
                r_sc[h] = later + total

    tile(kt_diag, scores(kt_diag), True, 0)

    def body(j, carry):
        slot = j % 2
        tile(kt_diag - 1 - j, [z_sc[slot, h] for h in heads], False, 1 - slot)
        return carry

    lax.fori_loop(0, kt_diag, body, 0)
    outs = [jnp.where(low, acc_sc[2 * p], acc_sc[2 * p + 1]) for p in range(N_PAIRS)]
    o_ref[0] = _rms(jnp.concatenate(outs, axis=1), g_ref[...]).astype(o_ref.dtype)


def _attn(qa, ka, va, seq, layer, bias, out_norm):
    nb, seq_pad, _ = qa.shape
    assert seq_pad % KEY_TILE == 0
    nq = pl.cdiv(seq, CHUNK)
    head_tile = lambda: pltpu.VMEM((N_HEADS, CHUNK, LANES), F32)
    return pl.pallas_call(
        _attn_body,
        grid=(nb, nq),
        in_specs=[pl.BlockSpec((1, CHUNK, WIDTH), lambda b, i: (b, i, 0)),
                  pl.BlockSpec((1, seq_pad, WIDTH), lambda b, i: (b, 0, 0)),
                  pl.BlockSpec((1, seq_pad, WIDTH), lambda b, i: (b, 0, 0)),
                  pl.BlockSpec(memory_space=pltpu.SMEM),
                  pl.BlockSpec((None, 1, WIDTH), lambda b, i: (layer, 0, 0))],
        out_specs=pl.BlockSpec((1, CHUNK, WIDTH), lambda b, i: (b, i, 0)),
        out_shape=jax.ShapeDtypeStruct((nb, seq, WIDTH), BF16),
        scratch_shapes=[pltpu.VMEM((N_HEADS, CHUNK, LANES), BF16),
                        pltpu.VMEM((2, N_HEADS, CHUNK, KEY_TILE), F32), head_tile(), head_tile()],
        compiler_params=pltpu.CompilerParams(dimension_semantics=("parallel", "arbitrary"),
                                             vmem_limit_bytes=VMEM_LIMIT),
        name="sb_attn",
    )(qa, ka, va, bias, out_norm)


def _decode_body(n_group, pt_ref, q_ref, kn_ref, vn_ref, bias_ref, g_ref, *rest):
    k_refs = rest[:n_group]
    v_refs = rest[n_group:2 * n_group]
    o_ref, r_sc, acc_sc = rest[2 * n_group:]
    j = pl.program_id(1)
    t_new = q_ref.shape[1]
    rows = N_HEADS * t_new
    page = CHUNK
    u = _suffix_matrix(page)
    t_bits, d_bits = t_new.bit_length() - 1, HEAD_DIM.bit_length() - 1
    own = (_iota((rows, WIDTH), 0) >> t_bits) == (_iota((rows, WIDTH), 1) >> d_bits)
    q = q_ref[0]
    qbd = jnp.where(own, jnp.concatenate([q] * N_HEADS, axis=0), 0.0).astype(BF16)
    bias = bias_ref[...]

    @pl.when(j == 0)
    def _():
        pad = jnp.zeros((page - t_new, WIDTH), F32)
        kb = jnp.concatenate([kn_ref[0], pad], axis=0).astype(BF16)
        vb = jnp.concatenate([vn_ref[0], pad], axis=0).astype(BF16)
        z = _dot_nt(qbd, kb) + bias
        visible = _iota((rows, page), 1) < (_iota((rows, page), 0) & (t_new - 1))
        p = jnp.where(visible, _sb_softplus(z), 0.0)
        w = jnp.where(visible, jnp.exp(z - _split_dot(p, u, 2)), 0.0)
        acc_sc[...] = _dot(w.astype(BF16), vb)
        r_sc[...] = jnp.broadcast_to(jnp.sum(p, axis=1, keepdims=True), (rows, LANES))

    pages = range(n_group)
    zs = [_dot(qbd, k_refs[g][...].astype(BF16)) + bias for g in pages]
    ps = [_sb_softplus(z) for z in zs]
    u2 = jnp.concatenate([u, u], axis=0)
    sufs = [_dot(jnp.concatenate(_split(p, 2), axis=1), u2) for p in ps]
    later = r_sc[...]
    ws = []
    for g in pages:
        ws.append(jnp.exp(zs[g] - later - sufs[g]).astype(BF16))
        later = later + jnp.sum(ps[g], axis=1, keepdims=True)
    acc = acc_sc[...]
    for g in pages:
        acc = acc + _dot_nt(ws[g], v_refs[g][...].astype(BF16))
    acc_sc[...] = acc
    r_sc[...] = later

    @pl.when(j == pl.num_programs(1) - 1)
    def _():
        a = jnp.where(own, acc, 0.0)
        o = a[0:t_new]
        for h in range(1, N_HEADS):
            o = o + a[h * t_new:(h + 1) * t_new]
        o_ref[0] = _rms(o, g_ref[...]).astype(o_ref.dtype)


def _decode(q, k_new, v_new, cache_k, cache_v, page_table, layer, bias_tile, out_norm):
    db, t_new, _ = q.shape
    n_pages = page_table.shape[1]
    page = cache_k.shape[3]
    assert page == CHUNK and t_new == 8
    n_group = _pick_tile(n_pages, 8, 1)
    rows = N_HEADS * t_new
    per_b = lambda b, j, pt: (b, 0, 0)

    def page_spec(g):
        return pl.BlockSpec((None, None, WIDTH, page),
                            lambda b, j, pt: (layer, pt[b, n_pages - 1 - (j * n_group + g)], 0, 0))

    grid_spec = pltpu.PrefetchScalarGridSpec(
        num_scalar_prefetch=1,
        grid=(db, n_pages // n_group),
        in_specs=[pl.BlockSpec((1, t_new, WIDTH), per_b), pl.BlockSpec((1, t_new, WIDTH), per_b),
                  pl.BlockSpec((1, t_new, WIDTH), per_b),
                  pl.BlockSpec((None, rows, LANES), lambda b, j, pt: (layer, 0, 0)),
                  pl.BlockSpec((None, 1, WIDTH), lambda b, j, pt: (layer, 0, 0))]
                 + [page_spec(g) for g in range(n_group)] * 2,
        out_specs=pl.BlockSpec((1, t_new, WIDTH), per_b),
        scratch_shapes=[pltpu.VMEM((rows, LANES), F32), pltpu.VMEM((rows, WIDTH), F32)],
    )
    return pl.pallas_call(
        functools.partial(_decode_body, n_group),
        grid_spec=grid_spec,
        out_shape=jax.ShapeDtypeStruct((db, t_new, WIDTH), F32),
        compiler_params=pltpu.CompilerParams(dimension_semantics=("parallel", "arbitrary"),
                                             vmem_limit_bytes=VMEM_LIMIT),
        name="sb_decode",
    )(page_table, q, k_new, v_new, bias_tile, out_norm, *([cache_k] * n_group), *([cache_v] * n_group))


def kernel(x_prompt, x_sample, cache_k, cache_v, state_ssm, state_conv, page_table, meta_tokens,
           norm_ffn1, ffn1_w_gu, ffn1_w_down, norm_mix, w_in, conv_w, conv_b, dt_bias, A_log,
           D_skip, ssd_norm, q_norm, k_norm, sb_bias, sb_out_norm, w_out, norm_ffn2, ffn2_w_gu,
           ffn2_w_down):
    bp, seq_in, d = x_prompt.shape
    db, t_new, _ = x_sample.shape
    depth = w_in.shape[0]
    n_meta = meta_tokens.shape[0]
    seq = n_meta + seq_in
    seq_pad = pl.cdiv(seq, KEY_TILE) * KEY_TILE

    wgu1, wd1 = ffn1_w_gu.astype(BF16), ffn1_w_down.astype(BF16)
    wgu2, wd2 = ffn2_w_gu.astype(BF16), ffn2_w_down.astype(BF16)
    wo = w_out.astype(BF16)
    o_xbc, o_dt = WIDTH + CONV_DIM, WIDTH + CONV_DIM + N_HEADS
    w_dt = jnp.pad(w_in[:, :, o_xbc:o_dt], ((0, 0), (0, 0), (0, DT_PAD - N_HEADS)))
    w_in_r = jnp.concatenate([w_in[:, :, :o_xbc], w_in[:, :, o_dt:], w_dt], axis=-1).astype(BF16)
    assert w_in_r.shape[-1] == _C_END
    vec = lambda a: a[:, None, :]
    norm_ffn1, norm_mix, norm_ffn2 = vec(norm_ffn1), vec(norm_mix), vec(norm_ffn2)
    qn = vec(jnp.tile(q_norm, (1, N_HEADS)))
    kn = vec(jnp.tile(k_norm, (1, N_HEADS)))
    ones_bd = ((jnp.arange(WIDTH)[:, None] // HEAD_DIM) == (jnp.arange(WIDTH)[None, :] // HEAD_DIM)).astype(BF16)
    expand = (jnp.arange(LANES)[:, None] == (jnp.arange(WIDTH)[None, :] // HEAD_DIM)).astype(BF16)
    cw = jnp.pad(conv_w, ((0, 0), (0, 8 - CONV_W), (0, 0)))
    conv_b = vec(conv_b)
    dtb = vec(jnp.pad(dt_bias, ((0, 0), (0, DT_PAD - N_HEADS))))
    alog = vec(jnp.pad(A_log, ((0, 0), (0, DT_PAD - N_HEADS))))
    dsk = vec(jnp.repeat(D_skip, HEAD_DIM, axis=1))
    ssd_norm, sb_out_norm = vec(ssd_norm), vec(sb_out_norm)
    bias_tile = jnp.broadcast_to(jnp.repeat(sb_bias, t_new, axis=1)[:, :, None], (depth, N_HEADS * t_new, LANES))
    n_phys, page = cache_k.shape[1], cache_k.shape[2]
    page_t = lambda c: jnp.transpose(c, (0, 1, 3, 4, 2)).reshape(depth, n_phys, WIDTH, page)
    ck, cv = page_t(cache_k), page_t(cache_v)

    meta = jnp.broadcast_to(meta_tokens[None].astype(x_prompt.dtype), (bp, n_meta, d))
    xp = jnp.concatenate([meta, x_prompt], axis=1).reshape(bp * seq, d)
    xs = x_sample.reshape(db * t_new, d)
    conv0_p = jnp.zeros((bp, 8, CONV_DIM), F32)
    state0_p = jnp.zeros((bp, LANES, WIDTH), F32)

    outs = {name: [] for name in ("sp", "cp", "ks", "vs", "ss", "cs")}
    kp_stack = vp_stack = None
    for l in range(depth):
        xp = _ffn(xp, l, norm_ffn1, wgu1, wd1)
        z, xbc, dt, kp_stack, vp_stack, qa, ka, va = _mixin(xp.reshape(bp, seq, d), l, norm_mix, w_in_r, qn, kn,
                                                            ones_bd, seq_pad, kv_stack=(depth, kp_stack, vp_stack))
        y_ssd, s_fin = _ssd(xbc, z, dt, seq, conv0_p, state0_p, l, cw, conv_b, dtb, alog, dsk, ssd_norm, expand)
        y_sb = _attn(qa, ka, va, seq, l, sb_bias[l], sb_out_norm)
        xp = _ffn(xp, l, norm_ffn2, wgu2, wd2,
                  mix=(y_ssd.reshape(bp * seq, WIDTH), y_sb.reshape(bp * seq, WIDTH), wo))
        outs["sp"].append(_blocks_to_state(s_fin))
        outs["cp"].append(xbc[:, seq - (CONV_W - 1):, :])

        xs = _ffn(xs, l, norm_ffn1, wgu1, wd1)
        z, xbc, dt, k, v, qa, _, _ = _mixin(xs.reshape(1, db * t_new, d), l, norm_mix, w_in_r, qn, kn, ones_bd,
                                            db * t_new)
        shp = lambda a: a.reshape(db, t_new, a.shape[-1])
        xbc, k, v = shp(xbc), shp(k), shp(v)
        chunk = lambda a: jnp.pad(shp(a), ((0, 0), (0, CHUNK - t_new), (0, 0)))
        conv0_s = jnp.pad(state_conv[l], ((0, 0), (8 - (CONV_W - 1), 0), (0, 0)))
        y_ssd, s_fin = _ssd(chunk(xbc), chunk(z), chunk(dt), t_new, conv0_s, _state_to_blocks(state_ssm[l]), l,
                            cw, conv_b, dtb, alog, dsk, ssd_norm, expand)
        y_ssd = y_ssd[:, :t_new]
        y_sb = _decode(shp(qa).astype(F32), k, v, ck, cv, page_table, l, bias_tile, sb_out_norm)
        xs = _ffn(xs, l, norm_ffn2, wgu2, wd2,
                  mix=(y_ssd.reshape(db * t_new, WIDTH), y_sb.reshape(db * t_new, WIDTH), wo))
        outs["ks"].append(k.reshape(db, t_new, N_HEADS, HEAD_DIM))
        outs["vs"].append(v.reshape(db, t_new, N_HEADS, HEAD_DIM))
        outs["ss"].append(_blocks_to_state(s_fin))
        outs["cs"].append(jnp.concatenate([state_conv[l], xbc], axis=1)[:, -(CONV_W - 1):, :])

    y_prompt = xp.reshape(bp, seq, d)[:, n_meta:]
    y_sample = xs.reshape(db, t_new, d)
    st = lambda name: jnp.stack(outs[name])
    to_thd = lambda a: jnp.transpose(a.reshape(depth, bp, N_HEADS, HEAD_DIM, seq), (0, 1, 4, 2, 3))
    return (y_prompt, y_sample, to_thd(kp_stack), to_thd(vp_stack), st("sp"), st("cp"),
            st("ks"), st("vs"), st("ss"), st("cs"))
```

```python
import functools

import jax
import jax.numpy as jnp
from jax import lax
from jax.experimental import pallas as pl
from jax.experimental.pallas import tpu as pltpu

F32 = jnp.float32
BF16 = jnp.bfloat16
EPS = 1e-6
LOG2E = 1.4426950408889634

LANES = 128
HEAD_DIM = 64
N_HEADS = 8
WIDTH = N_HEADS * HEAD_DIM
N_PAIRS = WIDTH // LANES
SSD_GROUPS = 2
CONV_W = 4
CONV_DIM = WIDTH + 2 * SSD_GROUPS * HEAD_DIM
DT_PAD = LANES
CHUNK = 128
SAMPLE_CHUNK = 16
KEY_TILE = 256
VMEM_LIMIT = 56 * 1024 * 1024


def _dot(a, b):
    return jnp.dot(a, b, preferred_element_type=F32)


def _dot_nt(a, b):
    return lax.dot_general(a, b, (((1,), (1,)), ((), ())), preferred_element_type=F32)


def _dot_tn(a, b):
    return lax.dot_general(a, b, (((0,), (0,)), ((), ())), preferred_element_type=F32)


def _split(a, passes):
    parts = []
    rem = a
    for p in range(passes):
        part = rem.astype(BF16)
        parts.append(part)
        if p + 1 < passes:
            rem = rem - part.astype(F32)
    return parts


def _split_dot(a, m, passes):
    out = None
    for part in _split(a, passes):
        term = _dot(part, m)
        out = term if out is None else out + term
    return out


def _rms(x, g):
    ms = jnp.mean(x * x, axis=-1, keepdims=True)
    return x * lax.rsqrt(ms + EPS) * g


def _silu(x):
    return x * jax.nn.sigmoid(x)


def _softplus(x):
    return jnp.maximum(x, 0.0) + jnp.log1p(jnp.exp(-jnp.abs(x)))


def _iota(shape, dim):
    return lax.broadcasted_iota(jnp.int32, shape, dim)


def _pick_tile(n, cap, mult):
    if n <= cap:
        return n
    best = None
    for t in range(mult, cap + 1, mult):
        if n % t == 0:
            best = t
    assert best is not None, (n, cap, mult)
    return best


def _resident(shape, index_map):
    return pl.BlockSpec(shape, index_map, pipeline_mode=pl.Buffered(1))


def _ffn_body(has_mix, n_chunks, fc, *refs):
    if has_mix:
        x_ref, ys_ref, yb_ref, wo_ref, g_ref, wgu_ref, wd_ref, o_ref = refs
        x = (x_ref[...]
             + _dot(ys_ref[...].astype(BF16), wo_ref[:WIDTH, :])
             + _dot(yb_ref[...].astype(BF16), wo_ref[WIDTH:, :]))
    else:
        x_ref, g_ref, wgu_ref, wd_ref, o_ref = refs
        x = x_ref[...]
    h = _rms(x, g_ref[...]).astype(BF16)
    d_ff = n_chunks * fc
    acc = None
    for c in range(n_chunks):
        gate = _dot(h, wgu_ref[:, c * fc:(c + 1) * fc])
        up = _dot(h, wgu_ref[:, d_ff + c * fc:d_ff + (c + 1) * fc])
        a = (_silu(gate) * up).astype(BF16)
        d = _dot(a, wd_ref[c * fc:(c + 1) * fc, :])
        acc = d if acc is None else acc + d
    o_ref[...] = x + 0.5 * acc


def _ffn(x, layer, norm, wgu, wd, mix=None):
    t, d = x.shape
    d_ff = wd.shape[1]
    fc = 256
    assert d_ff % fc == 0
    tm = _pick_tile(t, 704, 16)
    row = lambda i: (i, 0)
    lay3 = lambda i: (layer, 0, 0)
    in_specs = [pl.BlockSpec((tm, d), row)]
    args = [x]
    if mix is not None:
        ys, yb, wo = mix
        in_specs += [pl.BlockSpec((tm, WIDTH), row), pl.BlockSpec((tm, WIDTH), row),
                     _resident((None, 2 * WIDTH, d), lay3)]
        args += [ys, yb, wo]
    in_specs += [_resident((None, 1, d), lay3), _resident((None, d, 2 * d_ff), lay3),
                 _resident((None, d_ff, d), lay3)]
    args += [norm, wgu, wd]
    return pl.pallas_call(
        functools.partial(_ffn_body, mix is not None, d_ff // fc, fc),
        grid=(t // tm,),
        in_specs=in_specs,
        out_specs=pl.BlockSpec((tm, d), row),
        out_shape=jax.ShapeDtypeStruct((t, d), F32),
        compiler_params=pltpu.CompilerParams(dimension_semantics=("parallel",), vmem_limit_bytes=VMEM_LIMIT),
        name="ffn_mix" if mix is not None else "ffn",
    )(*args)


_C_Z, _C_XBC, _C_Q, _C_K, _C_V, _C_DT, _C_END = 0, 512, 1280, 1792, 2304, 2816, 2944


def _head_norm(t, gain, ones_bd):
    ms = _split_dot(t * t, ones_bd, 2) * (1.0 / HEAD_DIM)
    return t * lax.rsqrt(ms + EPS) * gain


def _mixin_body(seq, tl, kv_transposed, x_ref, g_ref, w_ref, qn_ref, kn_ref, bd_ref, *refs):
    z_ref, xbc_ref, dt_ref, k_ref, v_ref, qa_ref, ka_ref, va_ref = refs[-8:]
    j = pl.program_id(1)
    h = _rms(x_ref[0], g_ref[...]).astype(BF16)
    p = _dot(h, w_ref[...])
    z_ref[0] = p[:, _C_Z:_C_XBC]
    xbc_ref[0] = p[:, _C_XBC:_C_Q]
    dt_ref[0] = p[:, _C_DT:_C_END]
    bd = bd_ref[...]
    q = _head_norm(p[:, _C_Q:_C_K], qn_ref[...], bd)
    k = _head_norm(p[:, _C_K:_C_V], kn_ref[...], bd)
    v = p[:, _C_V:_C_DT]
    if kv_transposed:
        k_t, v_t = k.T, v.T
        for s in range(k_ref.shape[0]):
            k_ref[s, 0] = k_t
            v_ref[s, 0] = v_t
    else:
        k_ref[0] = k
        v_ref[0] = v
    valid = (j * tl + _iota((tl, 1), 0)) < seq
    qa_ref[0] = jnp.where(valid, q * (HEAD_DIM ** -0.5), 0.0).astype(BF16)
    ka_ref[0] = jnp.where(valid, k, 0.0).astype(BF16)
    va_ref[0] = jnp.where(valid, v, 0.0).astype(BF16)


def _mixin(x, layer, norm, w_in, qn, kn, ones_bd, seq_pad, kv_stack=None):
    nb, seq, d = x.shape
    tl = _pick_tile(seq_pad, 512, 16 if kv_stack is None else LANES)
    blk = lambda b, j: (b, j, 0)
    lay3 = lambda b, j: (layer, 0, 0)
    f32_out = lambda c: jax.ShapeDtypeStruct((nb, seq, c), F32)
    att_out = jax.ShapeDtypeStruct((nb, seq_pad, WIDTH), BF16)
    args = [x, norm, w_in, qn, kn, ones_bd]
    in_specs = [pl.BlockSpec((1, tl, d), blk), _resident((None, 1, d), lay3),
                _resident((None, d, _C_END), lay3), _resident((None, 1, WIDTH), lay3),
                _resident((None, 1, WIDTH), lay3), _resident((WIDTH, WIDTH), lambda b, j: (0, 0))]
    aliases = {}
    if kv_stack is None:
        kv_spec = pl.BlockSpec((1, tl, WIDTH), blk)
        kv_out = f32_out(WIDTH)
    else:
        depth, k_stack, v_stack = kv_stack
        kv_out = jax.ShapeDtypeStruct((depth, nb, WIDTH, seq), F32)
        if k_stack is None:
            assert layer == 0
            kv_spec = pl.BlockSpec((depth, 1, WIDTH, tl), lambda b, j: (0, b, 0, j))
        else:
            kv_spec = pl.BlockSpec((1, 1, WIDTH, tl), lambda b, j: (layer, b, 0, j))
            aliases = {len(args): 3, len(args) + 1: 4}
            args += [k_stack, v_stack]
            in_specs += [pl.BlockSpec(memory_space=pl.ANY)] * 2
    return pl.pallas_call(
        functools.partial(_mixin_body, seq, tl, kv_stack is not None),
        grid=(nb, seq_pad // tl),
        in_specs=in_specs,
        out_specs=[pl.BlockSpec((1, tl, WIDTH), blk), pl.BlockSpec((1, tl, CONV_DIM), blk),
                   pl.BlockSpec((1, tl, DT_PAD), blk), kv_spec, kv_spec, pl.BlockSpec((1, tl, WIDTH), blk),
                   pl.BlockSpec((1, tl, WIDTH), blk), pl.BlockSpec((1, tl, WIDTH), blk)],
        out_shape=[f32_out(WIDTH), f32_out(CONV_DIM), f32_out(DT_PAD), kv_out, kv_out,
                   att_out, att_out, att_out],
        input_output_aliases=aliases,
        compiler_params=pltpu.CompilerParams(dimension_semantics=("parallel", "parallel"),
                                             vmem_limit_bytes=VMEM_LIMIT),
        name="mix_in",
    )(*args)


def _ssd_body(seq, q, xbc_ref, z_ref, dt_ref, cinit_ref, sinit_ref, cw_ref, cb_ref, dtb_ref, alog_ref,
              dsk_ref, nw_ref, exp_ref, y_ref, sout_ref, cbuf, st):
    c = pl.program_id(1)

    @pl.when(c == 0)
    def _():
        cbuf[0:8, :] = cinit_ref[0]
        st[...] = sinit_ref[0]

    valid = (c * q + _iota((q, 1), 0)) < seq
    xraw = jnp.where(valid, xbc_ref[0], 0.0)
    cbuf[8:8 + q, :] = xraw
    cw = cw_ref[...]
    conv = (cbuf[5:5 + q, :] * cw[0:1] + cbuf[6:6 + q, :] * cw[1:2]
            + cbuf[7:7 + q, :] * cw[2:3] + xraw * cw[3:4])
    cbuf[5:8, :] = cbuf[5 + q:8 + q, :]
    xc = _silu(conv + cb_ref[...])
    xs = xc[:, :WIDTH]
    bm = xc[:, WIDTH:WIDTH + LANES].astype(BF16)
    cm = xc[:, WIDTH + LANES:].astype(BF16)

    dt = jnp.where(valid, _softplus(dt_ref[0] + dtb_ref[...]), 0.0)
    a = dt * (-jnp.exp(alog_ref[...]))
    tri = jnp.where(_iota((q, q), 0) >= _iota((q, q), 1), 1.0, 0.0).astype(BF16)
    cs = None
    for part in _split(a, 3):
        term = _dot(tri, part)
        cs = term if cs is None else cs + term
    ecs = jnp.exp(cs)
    to_end = jnp.exp(cs[q - 1:q, :] - cs)
    expand = exp_ref[...]
    dt_e = _split_dot(dt, expand, 2)
    ecs_e = _split_dot(ecs, expand, 2)
    to_end_e = _split_dot(to_end, expand, 2)
    xdt = xs * dt_e

    lane = _iota((q, LANES), 1)
    low = lane < HEAD_DIM
    g0 = _dot_nt(jnp.where(low, cm, jnp.zeros_like(cm)), bm)
    g1 = _dot_nt(jnp.where(low, jnp.zeros_like(cm), cm), bm)
    cs_t = cs.T
    causal = _iota((q, q), 0) >= _iota((q, q), 1)
    xdt_b = xdt.astype(BF16)
    y_pairs = []
    for p in range(N_PAIRS):
        xp = xdt_b[:, p * LANES:(p + 1) * LANES]
        yp = None
        for h, xm in ((2 * p, jnp.where(low, xp, jnp.zeros_like(xp))),
                      (2 * p + 1, jnp.where(low, jnp.zeros_like(xp), xp))):
            seg = cs[:, h:h + 1] - cs_t[h:h + 1, :]
            decay = jnp.exp(jnp.where(causal, seg, -jnp.inf))
            m = ((g0 if h < N_HEADS // SSD_GROUPS else g1) * decay).astype(BF16)
            t = _dot(m, xm)
            yp = t if yp is None else yp + t
        y_pairs.append(yp)
    y_diag = jnp.concatenate(y_pairs, axis=1)

    s_prev = st[...]
    y_off = _dot(cm, s_prev.astype(BF16)) * ecs_e
    y = y_diag + y_off + dsk_ref[...] * xs
    zz = jnp.where(valid, z_ref[0], 0.0)
    y = y * _silu(zz)
    half = WIDTH // SSD_GROUPS
    y0, y1 = y[:, :half], y[:, half:]
    y0 = y0 * lax.rsqrt(jnp.mean(y0 * y0, axis=-1, keepdims=True) + EPS)
    y1 = y1 * lax.rsqrt(jnp.mean(y1 * y1, axis=-1, keepdims=True) + EPS)
    y_ref[0] = (jnp.concatenate([y0, y1], axis=1) * nw_ref[...]).astype(y_ref.dtype)

    upd = _dot_tn(bm, (xdt * to_end_e).astype(BF16))
    own = (_iota((LANES, WIDTH), 0) < HEAD_DIM) == (_iota((LANES, WIDTH), 1) < half)
    st[...] = s_prev * ecs_e[q - 1:q, :] + jnp.where(own, upd, 0.0)

    @pl.when(c == pl.num_programs(1) - 1)
    def _():
        sout_ref[0] = st[...]


def _ssd(xbc, z, dt, seq, conv_init, state_init, layer, cw, cb, dtb, alog, dsk, nw, expand, q=CHUNK):
    nb, n_rows, _ = xbc.shape
    nchunk = pl.cdiv(n_rows, q)
    blk = lambda b, c: (b, c, 0)
    per_b = lambda b, c: (b, 0, 0)
    lay3 = lambda b, c: (layer, 0, 0)
    return pl.pallas_call(
        functools.partial(_ssd_body, seq, q),
        grid=(nb, nchunk),
        in_specs=[pl.BlockSpec((1, q, CONV_DIM), blk), pl.BlockSpec((1, q, WIDTH), blk),
                  pl.BlockSpec((1, q, DT_PAD), blk), pl.BlockSpec((1, 8, CONV_DIM), per_b),
                  pl.BlockSpec((1, LANES, WIDTH), per_b), pl.BlockSpec((None, 8, CONV_DIM), lay3),
                  pl.BlockSpec((None, 1, CONV_DIM), lay3), pl.BlockSpec((None, 1, DT_PAD), lay3),
                  pl.BlockSpec((None, 1, DT_PAD), lay3), pl.BlockSpec((None, 1, WIDTH), lay3),
                  pl.BlockSpec((None, 1, WIDTH), lay3), pl.BlockSpec((LANES, WIDTH), lambda b, c: (0, 0))],
        out_specs=[pl.BlockSpec((1, q, WIDTH), blk), pl.BlockSpec((1, LANES, WIDTH), per_b)],
        out_shape=[jax.ShapeDtypeStruct((nb, n_rows, WIDTH), BF16),
                   jax.ShapeDtypeStruct((nb, LANES, WIDTH), F32)],
        scratch_shapes=[pltpu.VMEM((8 + q, CONV_DIM), F32), pltpu.VMEM((LANES, WIDTH), F32)],
        compiler_params=pltpu.CompilerParams(dimension_semantics=("parallel", "arbitrary"),
                                             vmem_limit_bytes=VMEM_LIMIT),
        name="ssd",
    )(xbc, z, dt, conv_init, state_init, cw, cb, dtb, alog, dsk, nw, expand)


def _state_to_blocks(s):
    nb = s.shape[0]
    hpg = N_HEADS // SSD_GROUPS
    t = jnp.transpose(s.reshape(nb, SSD_GROUPS, hpg, HEAD_DIM, HEAD_DIM), (0, 1, 4, 2, 3))
    t = t.reshape(nb, SSD_GROUPS, HEAD_DIM, hpg * HEAD_DIM)
    zero = jnp.zeros_like(t[:, 0])
    top = jnp.concatenate([t[:, 0], zero], axis=-1)
    bot = jnp.concatenate([zero, t[:, 1]], axis=-1)
    return jnp.concatenate([top, bot], axis=1)


def _blocks_to_state(sb):
    nb = sb.shape[0]
    hpg = N_HEADS // SSD_GROUPS
    half = WIDTH // SSD_GROUPS
    t = jnp.stack([sb[:, :HEAD_DIM, :half], sb[:, HEAD_DIM:, half:]], axis=1)
    t = t.reshape(nb, SSD_GROUPS, HEAD_DIM, hpg, HEAD_DIM)
    return jnp.transpose(t, (0, 1, 3, 4, 2)).reshape(nb, N_HEADS, HEAD_DIM, HEAD_DIM)


def _sb_softplus(z):
    return jnp.maximum(z, 0.0) + jnp.log(1.0 + jnp.exp2(jnp.abs(z) * (-LOG2E)))


def _suffix_matrix(tk):
    return jnp.where(_iota((tk, tk), 0) >= _iota((tk, tk), 1), 1.0, 0.0).astype(BF16)


def _attn_body(q_ref, k_ref, v_ref, bias_ref, g_ref, o_ref, qm_sc, z_sc, r_sc, acc_sc):
    i = pl.program_id(1)
    tq, tk = CHUNK, KEY_TILE
    low = _iota((tq, LANES), 1) < HEAD_DIM
    for p in range(N_PAIRS):
        qp = q_ref[0, :, p * LANES:(p + 1) * LANES]
        qm_sc[2 * p] = jnp.where(low, qp, jnp.zeros_like(qp))
        qm_sc[2 * p + 1] = jnp.where(low, jnp.zeros_like(qp), qp)
    u = _suffix_matrix(tk)
    u2 = jnp.concatenate([u, u], axis=0)
    kt_diag = i // 2
    visible = _iota((tq, tk), 1) < _iota((tq, tk), 0) + (i - 2 * kt_diag) * tq
    heads = range(N_HEADS)
    pair = lambda h: slice((h // 2) * LANES, (h // 2 + 1) * LANES)

    def scores(kt):
        ks = pl.multiple_of(kt * tk, tk)
        return [_dot_nt(qm_sc[h], k_ref[0, pl.ds(ks, tk), pair(h)]) + bias_ref[h] for h in heads]

    def tile(kt, zs, diagonal, next_slot):
        ks = pl.multiple_of(kt * tk, tk)
        ps = [_sb_softplus(z) for z in zs]
        if diagonal:
            ps = [jnp.where(visible, p, 0.0) for p in ps]
        sufs = [_dot(jnp.concatenate(_split(p, 2), axis=1), u2) for p in ps]
        ws = [jnp.exp(z - s) for z, s in zip(zs, sufs)]
        if diagonal:
            ws = [jnp.where(visible, w, 0.0) for w in ws]
        pvs = [_dot(ws[h].astype(BF16), v_ref[0, pl.ds(ks, tk), pair(h)]) for h in heads]
        for h, z in zip(heads, scores(jnp.maximum(kt - 1, 0))):
            z_sc[next_slot, h] = z
        for h in heads:
            total = jnp.sum(ps[h], axis=1, keepdims=True)
            if diagonal:
                acc_sc[h] = pvs[h]
                r_sc[h] = jnp.broadcast_to(total, (tq, LANES))
            else:
                later = r_sc[h]
                acc_sc[h] = acc_sc[h] + jnp.exp(-later) * pvs[h]
                r_sc[h] = later + total

    tile(kt_diag, scores(kt_diag), True, 0)

    def body(j, carry):
        slot = j % 2
        tile(kt_diag - 1 - j, [z_sc[slot, h] for h in heads], False, 1 - slot)
        return carry

    lax.fori_loop(0, kt_diag, body, 0)
    outs = [jnp.where(low, acc_sc[2 * p], acc_sc[2 * p + 1]) for p in range(N_PAIRS)]
    o_ref[0] = _rms(jnp.concatenate(outs, axis=1), g_ref[...]).astype(o_ref.dtype)


def _attn(qa, ka, va, seq, layer, bias, out_norm):
    nb, seq_pad, _ = qa.shape
    assert seq_pad % KEY_TILE == 0
    nq = pl.cdiv(seq, CHUNK)
    head_tile = lambda: pltpu.VMEM((N_HEADS, CHUNK, LANES), F32)
    return pl.pallas_call(
        _attn_body,
        grid=(nb, nq),
        in_specs=[pl.BlockSpec((1, CHUNK, WIDTH), lambda b, i: (b, i, 0)),
                  pl.BlockSpec((1, seq_pad, WIDTH), lambda b, i: (b, 0, 0)),
                  pl.BlockSpec((1, seq_pad, WIDTH), lambda b, i: (b, 0, 0)),
                  pl.BlockSpec(memory_space=pltpu.SMEM),
                  pl.BlockSpec((None, 1, WIDTH), lambda b, i: (layer, 0, 0))],
        out_specs=pl.BlockSpec((1, CHUNK, WIDTH), lambda b, i: (b, i, 0)),
        out_shape=jax.ShapeDtypeStruct((nb, seq, WIDTH), BF16),
        scratch_shapes=[pltpu.VMEM((N_HEADS, CHUNK, LANES), BF16),
                        pltpu.VMEM((2, N_HEADS, CHUNK, KEY_TILE), F32), head_tile(), head_tile()],
        compiler_params=pltpu.CompilerParams(dimension_semantics=("parallel", "arbitrary"),
                                             vmem_limit_bytes=VMEM_LIMIT),
        name="sb_attn",
    )(qa, ka, va, bias, out_norm)


def _decode_body(n_group, pt_ref, q_ref, kn_ref, vn_ref, bias_ref, g_ref, *rest):
    k_refs = rest[:n_group]
    v_refs = rest[n_group:2 * n_group]
    o_ref, r_sc, acc_sc = rest[2 * n_group:]
    j = pl.program_id(1)
    t_new = q_ref.shape[1]
    rows = N_HEADS * t_new
    page = CHUNK
    u = _suffix_matrix(page)
    t_bits, d_bits = t_new.bit_length() - 1, HEAD_DIM.bit_length() - 1
    own = (_iota((rows, WIDTH), 0) >> t_bits) == (_iota((rows, WIDTH), 1) >> d_bits)
    q = q_ref[0]
    qbd = jnp.where(own, jnp.concatenate([q] * N_HEADS, axis=0), 0.0).astype(BF16)
    bias = bias_ref[...]

    @pl.when(j == 0)
    def _():
        pad = jnp.zeros((page - t_new, WIDTH), F32)
        kb = jnp.concatenate([kn_ref[0], pad], axis=0).astype(BF16)
        vb = jnp.concatenate([vn_ref[0], pad], axis=0).astype(BF16)
        z = _dot_nt(qbd, kb) + bias
        visible = _iota((rows, page), 1) < (_iota((rows, page), 0) & (t_new - 1))
        p = jnp.where(visible, _sb_softplus(z), 0.0)
        w = jnp.where(visible, jnp.exp(z - _split_dot(p, u, 2)), 0.0)
        acc_sc[...] = _dot(w.astype(BF16), vb)
        r_sc[...] = jnp.broadcast_to(jnp.sum(p, axis=1, keepdims=True), (rows, LANES))

    pages = range(n_group)
    zs = [_dot(qbd, k_refs[g][...].astype(BF16)) + bias for g in pages]
    ps = [_sb_softplus(z) for z in zs]
    u2 = jnp.concatenate([u, u], axis=0)
    sufs = [_dot(jnp.concatenate(_split(p, 2), axis=1), u2) for p in ps]
    later = r_sc[...]
    ws = []
    for g in pages:
        ws.append(jnp.exp(zs[g] - later - sufs[g]).astype(BF16))
        later = later + jnp.sum(ps[g], axis=1, keepdims=True)
    acc = acc_sc[...]
    for g in pages:
        acc = acc + _dot_nt(ws[g], v_refs[g][...].astype(BF16))
    acc_sc[...] = acc
    r_sc[...] = later

    @pl.when(j == pl.num_programs(1) - 1)
    def _():
        a = jnp.where(own, acc, 0.0)
        o = a[0:t_new]
        for h in range(1, N_HEADS):
            o = o + a[h * t_new:(h + 1) * t_new]
        o_ref[0] = _rms(o, g_ref[...]).astype(o_ref.dtype)


def _decode(q, k_new, v_new, cache_k, cache_v, page_table, layer, bias_tile, out_norm):
    db, t_new, _ = q.shape
    n_pages = page_table.shape[1]
    page = cache_k.shape[3]
    assert page == CHUNK and t_new == 8
    n_group = _pick_tile(n_pages, 16, 1)
    rows = N_HEADS * t_new
    per_b = lambda b, j, pt: (b, 0, 0)

    def page_spec(g):
        return pl.BlockSpec((None, None, WIDTH, page),
                            lambda b, j, pt: (layer, pt[b, n_pages - 1 - (j * n_group + g)], 0, 0))

    grid_spec = pltpu.PrefetchScalarGridSpec(
        num_scalar_prefetch=1,
        grid=(db, n_pages // n_group),
        in_specs=[pl.BlockSpec((1, t_new, WIDTH), per_b), pl.BlockSpec((1, t_new, WIDTH), per_b),
                  pl.BlockSpec((1, t_new, WIDTH), per_b),
                  pl.BlockSpec((None, rows, LANES), lambda b, j, pt: (layer, 0, 0)),
                  pl.BlockSpec((None, 1, WIDTH), lambda b, j, pt: (layer, 0, 0))]
                 + [page_spec(g) for g in range(n_group)] * 2,
        out_specs=pl.BlockSpec((1, t_new, WIDTH), per_b),
        scratch_shapes=[pltpu.VMEM((rows, LANES), F32), pltpu.VMEM((rows, WIDTH), F32)],
    )
    return pl.pallas_call(
        functools.partial(_decode_body, n_group),
        grid_spec=grid_spec,
        out_shape=jax.ShapeDtypeStruct((db, t_new, WIDTH), F32),
        compiler_params=pltpu.CompilerParams(dimension_semantics=("parallel", "arbitrary"),
                                             vmem_limit_bytes=VMEM_LIMIT),
        name="sb_decode",
    )(page_table, q, k_new, v_new, bias_tile, out_norm, *([cache_k] * n_group), *([cache_v] * n_group))


def kernel(x_prompt, x_sample, cache_k, cache_v, state_ssm, state_conv, page_table, meta_tokens,
           norm_ffn1, ffn1_w_gu, ffn1_w_down, norm_mix, w_in, conv_w, conv_b, dt_bias, A_log,
           D_skip, ssd_norm, q_norm, k_norm, sb_bias, sb_out_norm, w_out, norm_ffn2, ffn2_w_gu,
           ffn2_w_down):
    bp, seq_in, d = x_prompt.shape
    db, t_new, _ = x_sample.shape
    depth = w_in.shape[0]
    n_meta = meta_tokens.shape[0]
    seq = n_meta + seq_in
    seq_pad = pl.cdiv(seq, KEY_TILE) * KEY_TILE

    wgu1, wd1 = ffn1_w_gu.astype(BF16), ffn1_w_down.astype(BF16)
    wgu2, wd2 = ffn2_w_gu.astype(BF16), ffn2_w_down.astype(BF16)
    wo = w_out.astype(BF16)
    o_xbc, o_dt = WIDTH + CONV_DIM, WIDTH + CONV_DIM + N_HEADS
    w_dt = jnp.pad(w_in[:, :, o_xbc:o_dt], ((0, 0), (0, 0), (0, DT_PAD - N_HEADS)))
    w_in_r = jnp.concatenate([w_in[:, :, :o_xbc], w_in[:, :, o_dt:], w_dt], axis=-1).astype(BF16)
    assert w_in_r.shape[-1] == _C_END
    vec = lambda a: a[:, None, :]
    norm_ffn1, norm_mix, norm_ffn2 = vec(norm_ffn1), vec(norm_mix), vec(norm_ffn2)
    qn = vec(jnp.tile(q_norm, (1, N_HEADS)))
    kn = vec(jnp.tile(k_norm, (1, N_HEADS)))
    ones_bd = ((jnp.arange(WIDTH)[:, None] // HEAD_DIM) == (jnp.arange(WIDTH)[None, :] // HEAD_DIM)).astype(BF16)
    expand = (jnp.arange(LANES)[:, None] == (jnp.arange(WIDTH)[None, :] // HEAD_DIM)).astype(BF16)
    cw = jnp.pad(conv_w, ((0, 0), (0, 8 - CONV_W), (0, 0)))
    conv_b = vec(conv_b)
    dtb = vec(jnp.pad(dt_bias, ((0, 0), (0, DT_PAD - N_HEADS))))
    alog = vec(jnp.pad(A_log, ((0, 0), (0, DT_PAD - N_HEADS))))
    dsk = vec(jnp.repeat(D_skip, HEAD_DIM, axis=1))
    ssd_norm, sb_out_norm = vec(ssd_norm), vec(sb_out_norm)
    bias_tile = jnp.broadcast_to(jnp.repeat(sb_bias, t_new, axis=1)[:, :, None], (depth, N_HEADS * t_new, LANES))
    n_phys, page = cache_k.shape[1], cache_k.shape[2]
    page_t = lambda c: jnp.transpose(c, (0, 1, 3, 4, 2)).reshape(depth, n_phys, WIDTH, page)
    ck, cv = page_t(cache_k), page_t(cache_v)

    meta = jnp.broadcast_to(meta_tokens[None].astype(x_prompt.dtype), (bp, n_meta, d))
    xp = jnp.concatenate([meta, x_prompt], axis=1).reshape(bp * seq, d)
    xs = x_sample.reshape(db * t_new, d)
    conv0_p = jnp.zeros((bp, 8, CONV_DIM), F32)
    state0_p = jnp.zeros((bp, LANES, WIDTH), F32)

    outs = {name: [] for name in ("sp", "cp", "ks", "vs", "ss", "cs")}
    kp_stack = vp_stack = None
    for l in range(depth):
        xp = _ffn(xp, l, norm_ffn1, wgu1, wd1)
        z, xbc, dt, kp_stack, vp_stack, qa, ka, va = _mixin(xp.reshape(bp, seq, d), l, norm_mix, w_in_r, qn, kn,
                                                            ones_bd, seq_pad, kv_stack=(depth, kp_stack, vp_stack))
        y_ssd, s_fin = _ssd(xbc, z, dt, seq, conv0_p, state0_p, l, cw, conv_b, dtb, alog, dsk, ssd_norm, expand)
        y_sb = _attn(qa, ka, va, seq, l, sb_bias[l], sb_out_norm)
        xp = _ffn(xp, l, norm_ffn2, wgu2, wd2,
                  mix=(y_ssd.reshape(bp * seq, WIDTH), y_sb.reshape(bp * seq, WIDTH), wo))
        outs["sp"].append(_blocks_to_state(s_fin))
        outs["cp"].append(xbc[:, seq - (CONV_W - 1):, :])

        xs = _ffn(xs, l, norm_ffn1, wgu1, wd1)
        z, xbc, dt, k, v, qa, _, _ = _mixin(xs.reshape(1, db * t_new, d), l, norm_mix, w_in_r, qn, kn, ones_bd,
                                            db * t_new)
        shp = lambda a: a.reshape(db, t_new, a.shape[-1])
        xbc, k, v = shp(xbc), shp(k), shp(v)
        chunk = lambda a: jnp.pad(shp(a), ((0, 0), (0, SAMPLE_CHUNK - t_new), (0, 0)))
        conv0_s = jnp.pad(state_conv[l], ((0, 0), (8 - (CONV_W - 1), 0), (0, 0)))
        y_ssd, s_fin = _ssd(chunk(xbc), chunk(z), chunk(dt), t_new, conv0_s, _state_to_blocks(state_ssm[l]), l,
                            cw, conv_b, dtb, alog, dsk, ssd_norm, expand, q=SAMPLE_CHUNK)
        y_ssd = y_ssd[:, :t_new]
        y_sb = _decode(shp(qa).astype(F32), k, v, ck, cv, page_table, l, bias_tile, sb_out_norm)
        xs = _ffn(xs, l, norm_ffn2, wgu2, wd2,
                  mix=(y_ssd.reshape(db * t_new, WIDTH), y_sb.reshape(db * t_new, WIDTH), wo))
        outs["ks"].append(k.reshape(db, t_new, N_HEADS, HEAD_DIM))
        outs["vs"].append(v.reshape(db, t_new, N_HEADS, HEAD_DIM))
        outs["ss"].append(_blocks_to_state(s_fin))
        outs["cs"].append(jnp.concatenate([state_conv[l], xbc], axis=1)[:, -(CONV_W - 1):, :])

    y_prompt = xp.reshape(bp, seq, d)[:, n_meta:]
    y_sample = xs.reshape(db, t_new, d)
    st = lambda name: jnp.stack(outs[name])
    to_thd = lambda a: jnp.transpose(a.reshape(depth, bp, N_HEADS, HEAD_DIM, seq), (0, 1, 4, 2, 3))
    return (y_prompt, y_sample, to_thd(kp_stack), to_thd(vp_stack), st("sp"), st("cp"),
            st("ks"), st("vs"), st("ss"), st("cs"))
```

```python
import functools

import jax
import jax.numpy as jnp
from jax import lax
from jax.experimental import pallas as pl
from jax.experimental.pallas import tpu as pltpu

F32 = jnp.float32
BF16 = jnp.bfloat16
EPS = 1e-6
LOG2E = 1.4426950408889634

LANES = 128
HEAD_DIM = 64
N_HEADS = 8
WIDTH = N_HEADS * HEAD_DIM
N_PAIRS = WIDTH // LANES
SSD_GROUPS = 2
CONV_W = 4
CONV_DIM = WIDTH + 2 * SSD_GROUPS * HEAD_DIM
DT_PAD = LANES
CHUNK = 128
SAMPLE_CHUNK = 16
KEY_TILE = 256
VMEM_LIMIT = 56 * 1024 * 1024


def _dot(a, b):
    return jnp.dot(a, b, preferred_element_type=F32)


def _dot_nt(a, b):
    return lax.dot_general(a, b, (((1,), (1,)), ((), ())), preferred_element_type=F32)


def _dot_tn(a, b):
    return lax.dot_general(a, b, (((0,), (0,)), ((), ())), preferred_element_type=F32)


def _split(a, passes):
    parts = []
    rem = a
    for p in range(passes):
        part = rem.astype(BF16)
        parts.append(part)
        if p + 1 < passes:
            rem = rem - part.astype(F32)
    return parts


def _split_dot(a, m, passes):
    out = None
    for part in _split(a, passes):
        term = _dot(part, m)
        out = term if out is None else out + term
    return out


def _rms(x, g):
    ms = jnp.mean(x * x, axis=-1, keepdims=True)
    return x * lax.rsqrt(ms + EPS) * g


def _silu(x):
    return x * jax.nn.sigmoid(x)


def _softplus(x):
    return jnp.maximum(x, 0.0) + jnp.log1p(jnp.exp(-jnp.abs(x)))


def _iota(shape, dim):
    return lax.broadcasted_iota(jnp.int32, shape, dim)


def _pick_tile(n, cap, mult):
    if n <= cap:
        return n
    best = None
    for t in range(mult, cap + 1, mult):
        if n % t == 0:
            best = t
    assert best is not None, (n, cap, mult)
    return best


def _resident(shape, index_map):
    return pl.BlockSpec(shape, index_map, pipeline_mode=pl.Buffered(1))


def _ffn_body(has_mix, n_chunks, fc, *refs):
    if has_mix:
        x_ref, ys_ref, yb_ref, wo_ref, g_ref, wgu_ref, wd_ref, o_ref = refs
        x = (x_ref[...]
             + _dot(ys_ref[...].astype(BF16), wo_ref[:WIDTH, :])
             + _dot(yb_ref[...].astype(BF16), wo_ref[WIDTH:, :]))
    else:
        x_ref, g_ref, wgu_ref, wd_ref, o_ref = refs
        x = x_ref[...]
    h = _rms(x, g_ref[...]).astype(BF16)
    d_ff = n_chunks * fc
    acc = None
    for c in range(n_chunks):
        gate = _dot(h, wgu_ref[:, c * fc:(c + 1) * fc])
        up = _dot(h, wgu_ref[:, d_ff + c * fc:d_ff + (c + 1) * fc])
        a = (_silu(gate) * up).astype(BF16)
        d = _dot(a, wd_ref[c * fc:(c + 1) * fc, :])
        acc = d if acc is None else acc + d
    o_ref[...] = x + 0.5 * acc


def _ffn(x, layer, norm, wgu, wd, mix=None):
    t, d = x.shape
    d_ff = wd.shape[1]
    fc = 256
    assert d_ff % fc == 0
    tm = _pick_tile(t, 704, 16)
    row = lambda i: (i, 0)
    lay3 = lambda i: (layer, 0, 0)
    in_specs = [pl.BlockSpec((tm, d), row)]
    args = [x]
    if mix is not None:
        ys, yb, wo = mix
        in_specs += [pl.BlockSpec((tm, WIDTH), row), pl.BlockSpec((tm, WIDTH), row),
                     _resident((None, 2 * WIDTH, d), lay3)]
        args += [ys, yb, wo]
    in_specs += [_resident((None, 1, d), lay3), _resident((None, d, 2 * d_ff), lay3),
                 _resident((None, d_ff, d), lay3)]
    args += [norm, wgu, wd]
    return pl.pallas_call(
        functools.partial(_ffn_body, mix is not None, d_ff // fc, fc),
        grid=(t // tm,),
        in_specs=in_specs,
        out_specs=pl.BlockSpec((tm, d), row),
        out_shape=jax.ShapeDtypeStruct((t, d), F32),
        compiler_params=pltpu.CompilerParams(dimension_semantics=("parallel",), vmem_limit_bytes=VMEM_LIMIT),
        name="ffn_mix" if mix is not None else "ffn",
    )(*args)


_C_Z, _C_XBC, _C_Q, _C_K, _C_V, _C_DT, _C_END = 0, 512, 1280, 1792, 2304, 2816, 2944


def _head_norm(t, gain, ones_bd):
    ms = _split_dot(t * t, ones_bd, 2) * (1.0 / HEAD_DIM)
    return t * lax.rsqrt(ms + EPS) * gain


def _mixin_body(seq, tl, kv_transposed, x_ref, g_ref, w_ref, qn_ref, kn_ref, bd_ref, *refs):
    z_ref, xbc_ref, dt_ref, k_ref, v_ref, qa_ref, ka_ref, va_ref = refs[-8:]
    j = pl.program_id(1)
    h = _rms(x_ref[0], g_ref[...]).astype(BF16)
    p = _dot(h, w_ref[...])
    z_ref[0] = p[:, _C_Z:_C_XBC]
    xbc_ref[0] = p[:, _C_XBC:_C_Q]
    dt_ref[0] = p[:, _C_DT:_C_END]
    bd = bd_ref[...]
    q = _head_norm(p[:, _C_Q:_C_K], qn_ref[...], bd)
    k = _head_norm(p[:, _C_K:_C_V], kn_ref[...], bd)
    v = p[:, _C_V:_C_DT]
    if kv_transposed:
        k_t, v_t = k.T, v.T
        for s in range(k_ref.shape[0]):
            k_ref[s, 0] = k_t
            v_ref[s, 0] = v_t
    else:
        k_ref[0] = k
        v_ref[0] = v
    valid = (j * tl + _iota((tl, 1), 0)) < seq
    qa_ref[0] = jnp.where(valid, q * (HEAD_DIM ** -0.5), 0.0).astype(BF16)
    ka_ref[0] = jnp.where(valid, k, 0.0).astype(BF16)
    va_ref[0] = jnp.where(valid, v, 0.0).astype(BF16)


def _mixin(x, layer, norm, w_in, qn, kn, ones_bd, seq_pad, kv_stack=None):
    nb, seq, d = x.shape
    tl = _pick_tile(seq_pad, 768, 16 if kv_stack is None else LANES)
    blk = lambda b, j: (b, j, 0)
    lay3 = lambda b, j: (layer, 0, 0)
    f32_out = lambda c: jax.ShapeDtypeStruct((nb, seq, c), F32)
    att_out = jax.ShapeDtypeStruct((nb, seq_pad, WIDTH), BF16)
    args = [x, norm, w_in, qn, kn, ones_bd]
    in_specs = [pl.BlockSpec((1, tl, d), blk), _resident((None, 1, d), lay3),
                _resident((None, d, _C_END), lay3), _resident((None, 1, WIDTH), lay3),
                _resident((None, 1, WIDTH), lay3), _resident((WIDTH, WIDTH), lambda b, j: (0, 0))]
    aliases = {}
    if kv_stack is None:
        kv_spec = pl.BlockSpec((1, tl, WIDTH), blk)
        kv_out = f32_out(WIDTH)
    else:
        depth, k_stack, v_stack = kv_stack
        kv_out = jax.ShapeDtypeStruct((depth, nb, WIDTH, seq), F32)
        if k_stack is None:
            assert layer == 0
            kv_spec = pl.BlockSpec((depth, 1, WIDTH, tl), lambda b, j: (0, b, 0, j))
        else:
            kv_spec = pl.BlockSpec((1, 1, WIDTH, tl), lambda b, j: (layer, b, 0, j))
            aliases = {len(args): 3, len(args) + 1: 4}
            args += [k_stack, v_stack]
            in_specs += [pl.BlockSpec(memory_space=pl.ANY)] * 2
    return pl.pallas_call(
        functools.partial(_mixin_body, seq, tl, kv_stack is not None),
        grid=(nb, seq_pad // tl),
        in_specs=in_specs,
        out_specs=[pl.BlockSpec((1, tl, WIDTH), blk), pl.BlockSpec((1, tl, CONV_DIM), blk),
                   pl.BlockSpec((1, tl, DT_PAD), blk), kv_spec, kv_spec, pl.BlockSpec((1, tl, WIDTH), blk),
                   pl.BlockSpec((1, tl, WIDTH), blk), pl.BlockSpec((1, tl, WIDTH), blk)],
        out_shape=[f32_out(WIDTH), f32_out(CONV_DIM), f32_out(DT_PAD), kv_out, kv_out,
                   att_out, att_out, att_out],
        input_output_aliases=aliases,
        compiler_params=pltpu.CompilerParams(dimension_semantics=("parallel", "parallel"),
                                             vmem_limit_bytes=VMEM_LIMIT),
        name="mix_in",
    )(*args)


def _ssd_body(seq, q, xbc_ref, z_ref, dt_ref, cinit_ref, sinit_ref, cw_ref, cb_ref, dtb_ref, alog_ref,
              dsk_ref, nw_ref, exp_ref, y_ref, sout_ref, cbuf, st):
    c = pl.program_id(1)

    @pl.when(c == 0)
    def _():
        cbuf[0:8, :] = cinit_ref[0]
        st[...] = sinit_ref[0]

    valid = (c * q + _iota((q, 1), 0)) < seq
    xraw = jnp.where(valid, xbc_ref[0], 0.0)
    cbuf[8:8 + q, :] = xraw
    cw = cw_ref[...]
    conv = (cbuf[5:5 + q, :] * cw[0:1] + cbuf[6:6 + q, :] * cw[1:2]
            + cbuf[7:7 + q, :] * cw[2:3] + xraw * cw[3:4])
    cbuf[5:8, :] = cbuf[5 + q:8 + q, :]
    xc = _silu(conv + cb_ref[...])
    xs = xc[:, :WIDTH]
    bm = xc[:, WIDTH:WIDTH + LANES].astype(BF16)
    cm = xc[:, WIDTH + LANES:].astype(BF16)

    dt = jnp.where(valid, _softplus(dt_ref[0] + dtb_ref[...]), 0.0)
    a = dt * (-jnp.exp(alog_ref[...]))
    tri = jnp.where(_iota((q, q), 0) >= _iota((q, q), 1), 1.0, 0.0).astype(BF16)
    cs = None
    for part in _split(a, 3):
        term = _dot(tri, part)
        cs = term if cs is None else cs + term
    ecs = jnp.exp(cs)
    to_end = jnp.exp(cs[q - 1:q, :] - cs)
    expand = exp_ref[...]
    dt_e = _split_dot(dt, expand, 2)
    ecs_e = _split_dot(ecs, expand, 2)
    to_end_e = _split_dot(to_end, expand, 2)
    xdt = xs * dt_e

    lane = _iota((q, LANES), 1)
    low = lane < HEAD_DIM
    g0 = _dot_nt(jnp.where(low, cm, jnp.zeros_like(cm)), bm)
    g1 = _dot_nt(jnp.where(low, jnp.zeros_like(cm), cm), bm)
    cs_t = cs.T
    causal = _iota((q, q), 0) >= _iota((q, q), 1)
    xdt_b = xdt.astype(BF16)
    y_pairs = []
    for p in range(N_PAIRS):
        xp = xdt_b[:, p * LANES:(p + 1) * LANES]
        yp = None
        for h, xm in ((2 * p, jnp.where(low, xp, jnp.zeros_like(xp))),
                      (2 * p + 1, jnp.where(low, jnp.zeros_like(xp), xp))):
            seg = cs[:, h:h + 1] - cs_t[h:h + 1, :]
            decay = jnp.exp(jnp.where(causal, seg, -jnp.inf))
            m = ((g0 if h < N_HEADS // SSD_GROUPS else g1) * decay).astype(BF16)
            t = _dot(m, xm)
            yp = t if yp is None else yp + t
        y_pairs.append(yp)
    y_diag = jnp.concatenate(y_pairs, axis=1)

    s_prev = st[...]
    y_off = _dot(cm, s_prev.astype(BF16)) * ecs_e
    y = y_diag + y_off + dsk_ref[...] * xs
    zz = jnp.where(valid, z_ref[0], 0.0)
    y = y * _silu(zz)
    half = WIDTH // SSD_GROUPS
    y0, y1 = y[:, :half], y[:, half:]
    y0 = y0 * lax.rsqrt(jnp.mean(y0 * y0, axis=-1, keepdims=True) + EPS)
    y1 = y1 * lax.rsqrt(jnp.mean(y1 * y1, axis=-1, keepdims=True) + EPS)
    y_ref[0] = (jnp.concatenate([y0, y1], axis=1) * nw_ref[...]).astype(y_ref.dtype)

    upd = _dot_tn(bm, (xdt * to_end_e).astype(BF16))
    own = (_iota((LANES, WIDTH), 0) < HEAD_DIM) == (_iota((LANES, WIDTH), 1) < half)
    st[...] = s_prev * ecs_e[q - 1:q, :] + jnp.where(own, upd, 0.0)

    @pl.when(c == pl.num_programs(1) - 1)
    def _():
        sout_ref[0] = st[...]


def _ssd(xbc, z, dt, seq, conv_init, state_init, layer, cw, cb, dtb, alog, dsk, nw, expand, q=CHUNK):
    nb, n_rows, _ = xbc.shape
    nchunk = pl.cdiv(n_rows, q)
    blk = lambda b, c: (b, c, 0)
    per_b = lambda b, c: (b, 0, 0)
    lay3 = lambda b, c: (layer, 0, 0)
    return pl.pallas_call(
        functools.partial(_ssd_body, seq, q),
        grid=(nb, nchunk),
        in_specs=[pl.BlockSpec((1, q, CONV_DIM), blk), pl.BlockSpec((1, q, WIDTH), blk),
                  pl.BlockSpec((1, q, DT_PAD), blk), pl.BlockSpec((1, 8, CONV_DIM), per_b),
                  pl.BlockSpec((1, LANES, WIDTH), per_b), pl.BlockSpec((None, 8, CONV_DIM), lay3),
                  pl.BlockSpec((None, 1, CONV_DIM), lay3), pl.BlockSpec((None, 1, DT_PAD), lay3),
                  pl.BlockSpec((None, 1, DT_PAD), lay3), pl.BlockSpec((None, 1, WIDTH), lay3),
                  pl.BlockSpec((None, 1, WIDTH), lay3), pl.BlockSpec((LANES, WIDTH), lambda b, c: (0, 0))],
        out_specs=[pl.BlockSpec((1, q, WIDTH), blk), pl.BlockSpec((1, LANES, WIDTH), per_b)],
        out_shape=[jax.ShapeDtypeStruct((nb, n_rows, WIDTH), BF16),
                   jax.ShapeDtypeStruct((nb, LANES, WIDTH), F32)],
        scratch_shapes=[pltpu.VMEM((8 + q, CONV_DIM), F32), pltpu.VMEM((LANES, WIDTH), F32)],
        compiler_params=pltpu.CompilerParams(dimension_semantics=("parallel", "arbitrary"),
                                             vmem_limit_bytes=VMEM_LIMIT),
        name="ssd",
    )(xbc, z, dt, conv_init, state_init, cw, cb, dtb, alog, dsk, nw, expand)


def _state_to_blocks(s):
    nb = s.shape[0]
    hpg = N_HEADS // SSD_GROUPS
    t = jnp.transpose(s.reshape(nb, SSD_GROUPS, hpg, HEAD_DIM, HEAD_DIM), (0, 1, 4, 2, 3))
    t = t.reshape(nb, SSD_GROUPS, HEAD_DIM, hpg * HEAD_DIM)
    zero = jnp.zeros_like(t[:, 0])
    top = jnp.concatenate([t[:, 0], zero], axis=-1)
    bot = jnp.concatenate([zero, t[:, 1]], axis=-1)
    return jnp.concatenate([top, bot], axis=1)


def _blocks_to_state(sb):
    nb = sb.shape[0]
    hpg = N_HEADS // SSD_GROUPS
    half = WIDTH // SSD_GROUPS
    t = jnp.stack([sb[:, :HEAD_DIM, :half], sb[:, HEAD_DIM:, half:]], axis=1)
    t = t.reshape(nb, SSD_GROUPS, HEAD_DIM, hpg, HEAD_DIM)
    return jnp.transpose(t, (0, 1, 3, 4, 2)).reshape(nb, N_HEADS, HEAD_DIM, HEAD_DIM)


def _sb_softplus(z):
    return jnp.maximum(z, 0.0) + jnp.log(1.0 + jnp.exp2(jnp.abs(z) * (-LOG2E)))


def _suffix_matrix(tk):
    return jnp.where(_iota((tk, tk), 0) >= _iota((tk, tk), 1), 1.0, 0.0).astype(BF16)


def _attn_body(q_ref, k_ref, v_ref, bias_ref, g_ref, o_ref, qm_sc, z_sc, r_sc, acc_sc):
    i = pl.program_id(1)
    tq, tk = CHUNK, KEY_TILE
    low = _iota((tq, LANES), 1) < HEAD_DIM
    for p in range(N_PAIRS):
        qp = q_ref[0, :, p * LANES:(p + 1) * LANES]
        qm_sc[2 * p] = jnp.where(low, qp, jnp.zeros_like(qp))
        qm_sc[2 * p + 1] = jnp.where(low, jnp.zeros_like(qp), qp)
    u = _suffix_matrix(tk)
    u2 = jnp.concatenate([u, u], axis=0)
    kt_diag = i // 2
    visible = _iota((tq, tk), 1) < _iota((tq, tk), 0) + (i - 2 * kt_diag) * tq
    heads = range(N_HEADS)
    pair = lambda h: slice((h // 2) * LANES, (h // 2 + 1) * LANES)

    def scores(kt):
        ks = pl.multiple_of(kt * tk, tk)
        return [_dot_nt(qm_sc[h], k_ref[0, pl.ds(ks, tk), pair(h)]) + bias_ref[h] for h in heads]

    def tile(kt, zs, diagonal, next_slot):
        ks = pl.multiple_of(kt * tk, tk)
        ps = [_sb_softplus(z) for z in zs]
        if diagonal:
            ps = [jnp.where(visible, p, 0.0) for p in ps]
        sufs = [_dot(jnp.concatenate(_split(p, 2), axis=1), u2) for p in ps]
        ws = [jnp.exp(z - s) for z, s in zip(zs, sufs)]
        if diagonal:
            ws = [jnp.where(visible, w, 0.0) for w in ws]
        pvs = [_dot(ws[h].astype(BF16), v_ref[0, pl.ds(ks, tk), pair(h)]) for h in heads]
        for h, z in zip(heads, scores(jnp.maximum(kt - 1, 0))):
            z_sc[next_slot, h] = z
        for h in heads:
            total = jnp.sum(ps[h], axis=1, keepdims=True)
            if diagonal:
                acc_sc[h] = pvs[h]
                r_sc[h] = jnp.broadcast_to(total, (tq, LANES))
            else:
                later = r_sc[h]
                acc_sc[h] = acc_sc[h] + jnp.exp(-later) * pvs[h]
                r_sc[h] = later + total

    tile(kt_diag, scores(kt_diag), True, 0)

    def body(j, carry):
        slot = j % 2
        tile(kt_diag - 1 - j, [z_sc[slot, h] for h in heads], False, 1 - slot)
        return carry

    lax.fori_loop(0, kt_diag, body, 0)
    outs = [jnp.where(low, acc_sc[2 * p], acc_sc[2 * p + 1]) for p in range(N_PAIRS)]
    o_ref[0] = _rms(jnp.concatenate(outs, axis=1), g_ref[...]).astype(o_ref.dtype)


def _attn(qa, ka, va, seq, layer, bias, out_norm):
    nb, seq_pad, _ = qa.shape
    assert seq_pad % KEY_TILE == 0
    nq = pl.cdiv(seq, CHUNK)
    head_tile = lambda: pltpu.VMEM((N_HEADS, CHUNK, LANES), F32)
    return pl.pallas_call(
        _attn_body,
        grid=(nb, nq),
        in_specs=[pl.BlockSpec((1, CHUNK, WIDTH), lambda b, i: (b, i, 0)),
                  pl.BlockSpec((1, seq_pad, WIDTH), lambda b, i: (b, 0, 0)),
                  pl.BlockSpec((1, seq_pad, WIDTH), lambda b, i: (b, 0, 0)),
                  pl.BlockSpec(memory_space=pltpu.SMEM),
                  pl.BlockSpec((None, 1, WIDTH), lambda b, i: (layer, 0, 0))],
        out_specs=pl.BlockSpec((1, CHUNK, WIDTH), lambda b, i: (b, i, 0)),
        out_shape=jax.ShapeDtypeStruct((nb, seq, WIDTH), BF16),
        scratch_shapes=[pltpu.VMEM((N_HEADS, CHUNK, LANES), BF16),
                        pltpu.VMEM((2, N_HEADS, CHUNK, KEY_TILE), F32), head_tile(), head_tile()],
        compiler_params=pltpu.CompilerParams(dimension_semantics=("parallel", "arbitrary"),
                                             vmem_limit_bytes=VMEM_LIMIT),
        name="sb_attn",
    )(qa, ka, va, bias, out_norm)


def _decode_body(n_group, pt_ref, q_ref, kn_ref, vn_ref, bias_ref, g_ref, *rest):
    k_refs = rest[:n_group]
    v_refs = rest[n_group:2 * n_group]
    o_ref, r_sc, acc_sc = rest[2 * n_group:]
    j = pl.program_id(1)
    t_new = q_ref.shape[1]
    rows = N_HEADS * t_new
    page = CHUNK
    u = _suffix_matrix(page)
    t_bits, d_bits = t_new.bit_length() - 1, HEAD_DIM.bit_length() - 1
    own = (_iota((rows, WIDTH), 0) >> t_bits) == (_iota((rows, WIDTH), 1) >> d_bits)
    q = q_ref[0]
    qbd = jnp.where(own, jnp.concatenate([q] * N_HEADS, axis=0), 0.0).astype(BF16)
    bias = bias_ref[...]

    @pl.when(j == 0)
    def _():
        pad = jnp.zeros((page - t_new, WIDTH), F32)
        kb = jnp.concatenate([kn_ref[0], pad], axis=0).astype(BF16)
        vb = jnp.concatenate([vn_ref[0], pad], axis=0).astype(BF16)
        z = _dot_nt(qbd, kb) + bias
        visible = _iota((rows, page), 1) < (_iota((rows, page), 0) & (t_new - 1))
        p = jnp.where(visible, _sb_softplus(z), 0.0)
        w = jnp.where(visible, jnp.exp(z - _split_dot(p, u, 2)), 0.0)
        acc_sc[...] = _dot(w.astype(BF16), vb)
        r_sc[...] = jnp.broadcast_to(jnp.sum(p, axis=1, keepdims=True), (rows, LANES))

    pages = range(n_group)
    zs = [_dot(qbd, k_refs[g][...].astype(BF16)) + bias for g in pages]
    ps = [_sb_softplus(z) for z in zs]
    u2 = jnp.concatenate([u, u], axis=0)
    sufs = [_dot(jnp.concatenate(_split(p, 2), axis=1), u2) for p in ps]
    later = r_sc[...]
    ws = []
    for g in pages:
        ws.append(jnp.exp(zs[g] - later - sufs[g]).astype(BF16))
        later = later + jnp.sum(ps[g], axis=1, keepdims=True)
    acc = acc_sc[...]
    for g in pages:
        acc = acc + _dot_nt(ws[g], v_refs[g][...].astype(BF16))
    acc_sc[...] = acc
    r_sc[...] = later

    @pl.when(j == pl.num_programs(1) - 1)
    def _():
        a = jnp.where(own, acc, 0.0)
        o = a[0:t_new]
        for h in range(1, N_HEADS):
            o = o + a[h * t_new:(h + 1) * t_new]
        o_ref[0] = _rms(o, g_ref[...]).astype(o_ref.dtype)


def _decode(q, k_new, v_new, cache_k, cache_v, page_table, layer, bias_tile, out_norm):
    db, t_new, _ = q.shape
    n_pages = page_table.shape[1]
    page = cache_k.shape[3]
    assert page == CHUNK and t_new == 8
    n_group = _pick_tile(n_pages, 32, 1)
    rows = N_HEADS * t_new
    per_b = lambda b, j, pt: (b, 0, 0)

    def page_spec(g):
        return pl.BlockSpec((None, None, WIDTH, page),
                            lambda b, j, pt: (layer, pt[b, n_pages - 1 - (j * n_group + g)], 0, 0))

    grid_spec = pltpu.PrefetchScalarGridSpec(
        num_scalar_prefetch=1,
        grid=(db, n_pages // n_group),
        in_specs=[pl.BlockSpec((1, t_new, WIDTH), per_b), pl.BlockSpec((1, t_new, WIDTH), per_b),
                  pl.BlockSpec((1, t_new, WIDTH), per_b),
                  pl.BlockSpec((None, rows, LANES), lambda b, j, pt: (layer, 0, 0)),
                  pl.BlockSpec((None, 1, WIDTH), lambda b, j, pt: (layer, 0, 0))]
                 + [page_spec(g) for g in range(n_group)] * 2,
        out_specs=pl.BlockSpec((1, t_new, WIDTH), per_b),
        scratch_shapes=[pltpu.VMEM((rows, LANES), F32), pltpu.VMEM((rows, WIDTH), F32)],
    )
    return pl.pallas_call(
        functools.partial(_decode_body, n_group),
        grid_spec=grid_spec,
        out_shape=jax.ShapeDtypeStruct((db, t_new, WIDTH), F32),
        compiler_params=pltpu.CompilerParams(dimension_semantics=("parallel", "arbitrary"),
                                             vmem_limit_bytes=VMEM_LIMIT),
        name="sb_decode",
    )(page_table, q, k_new, v_new, bias_tile, out_norm, *([cache_k] * n_group), *([cache_v] * n_group))


def kernel(x_prompt, x_sample, cache_k, cache_v, state_ssm, state_conv, page_table, meta_tokens,
           norm_ffn1, ffn1_w_gu, ffn1_w_down, norm_mix, w_in, conv_w, conv_b, dt_bias, A_log,
           D_skip, ssd_norm, q_norm, k_norm, sb_bias, sb_out_norm, w_out, norm_ffn2, ffn2_w_gu,
           ffn2_w_down):
    bp, seq_in, d = x_prompt.shape
    db, t_new, _ = x_sample.shape
    depth = w_in.shape[0]
    n_meta = meta_tokens.shape[0]
    seq = n_meta + seq_in
    seq_pad = pl.cdiv(seq, KEY_TILE) * KEY_TILE

    wgu1, wd1 = ffn1_w_gu.astype(BF16), ffn1_w_down.astype(BF16)
    wgu2, wd2 = ffn2_w_gu.astype(BF16), ffn2_w_down.astype(BF16)
    wo = w_out.astype(BF16)
    o_xbc, o_dt = WIDTH + CONV_DIM, WIDTH + CONV_DIM + N_HEADS
    w_dt = jnp.pad(w_in[:, :, o_xbc:o_dt], ((0, 0), (0, 0), (0, DT_PAD - N_HEADS)))
    w_in_r = jnp.concatenate([w_in[:, :, :o_xbc], w_in[:, :, o_dt:], w_dt], axis=-1).astype(BF16)
    assert w_in_r.shape[-1] == _C_END
    vec = lambda a: a[:, None, :]
    norm_ffn1, norm_mix, norm_ffn2 = vec(norm_ffn1), vec(norm_mix), vec(norm_ffn2)
    qn = vec(jnp.tile(q_norm, (1, N_HEADS)))
    kn = vec(jnp.tile(k_norm, (1, N_HEADS)))
    ones_bd = ((jnp.arange(WIDTH)[:, None] // HEAD_DIM) == (jnp.arange(WIDTH)[None, :] // HEAD_DIM)).astype(BF16)
    expand = (jnp.arange(LANES)[:, None] == (jnp.arange(WIDTH)[None, :] // HEAD_DIM)).astype(BF16)
    cw = jnp.pad(conv_w, ((0, 0), (0, 8 - CONV_W), (0, 0)))
    conv_b = vec(conv_b)
    dtb = vec(jnp.pad(dt_bias, ((0, 0), (0, DT_PAD - N_HEADS))))
    alog = vec(jnp.pad(A_log, ((0, 0), (0, DT_PAD - N_HEADS))))
    dsk = vec(jnp.repeat(D_skip, HEAD_DIM, axis=1))
    ssd_norm, sb_out_norm = vec(ssd_norm), vec(sb_out_norm)
    bias_tile = jnp.broadcast_to(jnp.repeat(sb_bias, t_new, axis=1)[:, :, None], (depth, N_HEADS * t_new, LANES))
    n_phys, page = cache_k.shape[1], cache_k.shape[2]
    page_t = lambda c: jnp.transpose(c, (0, 1, 3, 4, 2)).reshape(depth, n_phys, WIDTH, page)
    ck, cv = page_t(cache_k), page_t(cache_v)

    meta = jnp.broadcast_to(meta_tokens[None].astype(x_prompt.dtype), (bp, n_meta, d))
    xp = jnp.concatenate([meta, x_prompt], axis=1).reshape(bp * seq, d)
    xs = x_sample.reshape(db * t_new, d)
    conv0_p = jnp.zeros((bp, 8, CONV_DIM), F32)
    state0_p = jnp.zeros((bp, LANES, WIDTH), F32)

    outs = {name: [] for name in ("sp", "cp", "ks", "vs", "ss", "cs")}
    kp_stack = vp_stack = None
    for l in range(depth):
        xp = _ffn(xp, l, norm_ffn1, wgu1, wd1)
        z, xbc, dt, kp_stack, vp_stack, qa, ka, va = _mixin(xp.reshape(bp, seq, d), l, norm_mix, w_in_r, qn, kn,
                                                            ones_bd, seq_pad, kv_stack=(depth, kp_stack, vp_stack))
        y_ssd, s_fin = _ssd(xbc, z, dt, seq, conv0_p, state0_p, l, cw, conv_b, dtb, alog, dsk, ssd_norm, expand)
        y_sb = _attn(qa, ka, va, seq, l, sb_bias[l], sb_out_norm)
        xp = _ffn(xp, l, norm_ffn2, wgu2, wd2,
                  mix=(y_ssd.reshape(bp * seq, WIDTH), y_sb.reshape(bp * seq, WIDTH), wo))
        outs["sp"].append(_blocks_to_state(s_fin))
        outs["cp"].append(xbc[:, seq - (CONV_W - 1):, :])

        xs = _ffn(xs, l, norm_ffn1, wgu1, wd1)
        z, xbc, dt, k, v, qa, _, _ = _mixin(xs.reshape(1, db * t_new, d), l, norm_mix, w_in_r, qn, kn, ones_bd,
                                            db * t_new)
        shp = lambda a: a.reshape(db, t_new, a.shape[-1])
        xbc, k, v = shp(xbc), shp(k), shp(v)
        chunk = lambda a: jnp.pad(shp(a), ((0, 0), (0, SAMPLE_CHUNK - t_new), (0, 0)))
        conv0_s = jnp.pad(state_conv[l], ((0, 0), (8 - (CONV_W - 1), 0), (0, 0)))
        y_ssd, s_fin = _ssd(chunk(xbc), chunk(z), chunk(dt), t_new, conv0_s, _state_to_blocks(state_ssm[l]), l,
                            cw, conv_b, dtb, alog, dsk, ssd_norm, expand, q=SAMPLE_CHUNK)
        y_ssd = y_ssd[:, :t_new]
        y_sb = _decode(shp(qa).astype(F32), k, v, ck, cv, page_table, l, bias_tile, sb_out_norm)
        xs = _ffn(xs, l, norm_ffn2, wgu2, wd2,
                  mix=(y_ssd.reshape(db * t_new, WIDTH), y_sb.reshape(db * t_new, WIDTH), wo))
        outs["ks"].append(k.reshape(db, t_new, N_HEADS, HEAD_DIM))
        outs["vs"].append(v.reshape(db, t_new, N_HEADS, HEAD_DIM))
        outs["ss"].append(_blocks_to_state(s_fin))
        outs["cs"].append(jnp.concatenate([state_conv[l], xbc], axis=1)[:, -(CONV_W - 1):, :])

    y_prompt = xp.reshape(bp, seq, d)[:, n_meta:]
    y_sample = xs.reshape(db, t_new, d)
    st = lambda name: jnp.stack(outs[name])
    to_thd = lambda a: jnp.transpose(a.reshape(depth, bp, N_HEADS, HEAD_DIM, seq), (0, 1, 4, 2, 3))
    return (y_prompt, y_sample, to_thd(kp_stack), to_thd(vp_stack), st("sp"), st("cp"),
            st("ks"), st("vs"), st("ss"), st("cs"))
```

```python
import functools

import jax
import jax.numpy as jnp
from jax import lax
from jax.experimental import pallas as pl
from jax.experimental.pallas import tpu as pltpu

F32 = jnp.float32
BF16 = jnp.bfloat16
EPS = 1e-6
LOG2E = 1.4426950408889634

LANES = 128
HEAD_DIM = 64
N_HEADS = 8
WIDTH = N_HEADS * HEAD_DIM
N_PAIRS = WIDTH // LANES
SSD_GROUPS = 2
CONV_W = 4
CONV_DIM = WIDTH + 2 * SSD_GROUPS * HEAD_DIM
DT_PAD = LANES
CHUNK = 128
SAMPLE_CHUNK = 16
KEY_TILE = 256
VMEM_LIMIT = 56 * 1024 * 1024


def _dot(a, b):
    return jnp.dot(a, b, preferred_element_type=F32)


def _dot_nt(a, b):
    return lax.dot_general(a, b, (((1,), (1,)), ((), ())), preferred_element_type=F32)


def _dot_tn(a, b):
    return lax.dot_general(a, b, (((0,), (0,)), ((), ())), preferred_element_type=F32)


def _split(a, passes):
    parts = []
    rem = a
    for p in range(passes):
        part = rem.astype(BF16)
        parts.append(part)
        if p + 1 < passes:
            rem = rem - part.astype(F32)
    return parts


def _split_dot(a, m, passes):
    out = None
    for part in _split(a, passes):
        term = _dot(part, m)
        out = term if out is None else out + term
    return out


def _rms(x, g):
    ms = jnp.mean(x * x, axis=-1, keepdims=True)
    return x * lax.rsqrt(ms + EPS) * g


def _silu(x):
    return x * jax.nn.sigmoid(x)


def _softplus(x):
    return jnp.maximum(x, 0.0) + jnp.log1p(jnp.exp(-jnp.abs(x)))


def _iota(shape, dim):
    return lax.broadcasted_iota(jnp.int32, shape, dim)


def _pick_tile(n, cap, mult):
    if n <= cap:
        return n
    best = None
    for t in range(mult, cap + 1, mult):
        if n % t == 0:
            best = t
    assert best is not None, (n, cap, mult)
    return best


def _resident(shape, index_map):
    return pl.BlockSpec(shape, index_map, pipeline_mode=pl.Buffered(1))


def _ffn_body(has_mix, n_chunks, fc, *refs):
    if has_mix:
        x_ref, ys_ref, yb_ref, wo_ref, g_ref, wgu_ref, wd_ref, o_ref = refs
        x = (x_ref[...]
             + _dot(ys_ref[...].astype(BF16), wo_ref[:WIDTH, :])
             + _dot(yb_ref[...].astype(BF16), wo_ref[WIDTH:, :]))
    else:
        x_ref, g_ref, wgu_ref, wd_ref, o_ref = refs
        x = x_ref[...]
    h = _rms(x, g_ref[...]).astype(BF16)
    d_ff = n_chunks * fc
    acc = None
    for c in range(n_chunks):
        gate = _dot(h, wgu_ref[:, c * fc:(c + 1) * fc])
        up = _dot(h, wgu_ref[:, d_ff + c * fc:d_ff + (c + 1) * fc])
        a = (_silu(gate) * up).astype(BF16)
        d = _dot(a, wd_ref[c * fc:(c + 1) * fc, :])
        acc = d if acc is None else acc + d
    o_ref[...] = x + 0.5 * acc


def _ffn(x, layer, norm, wgu, wd, mix=None):
    t, d = x.shape
    d_ff = wd.shape[1]
    fc = 256
    assert d_ff % fc == 0
    tm = _pick_tile(t, 704, 16)
    row = lambda i: (i, 0)
    lay3 = lambda i: (layer, 0, 0)
    in_specs = [pl.BlockSpec((tm, d), row)]
    args = [x]
    if mix is not None:
        ys, yb, wo = mix
        in_specs += [pl.BlockSpec((tm, WIDTH), row), pl.BlockSpec((tm, WIDTH), row),
                     _resident((None, 2 * WIDTH, d), lay3)]
        args += [ys, yb, wo]
    in_specs += [_resident((None, 1, d), lay3), _resident((None, d, 2 * d_ff), lay3),
                 _resident((None, d_ff, d), lay3)]
    args += [norm, wgu, wd]
    return pl.pallas_call(
        functools.partial(_ffn_body, mix is not None, d_ff // fc, fc),
        grid=(t // tm,),
        in_specs=in_specs,
        out_specs=pl.BlockSpec((tm, d), row),
        out_shape=jax.ShapeDtypeStruct((t, d), F32),
        compiler_params=pltpu.CompilerParams(dimension_semantics=("parallel",), vmem_limit_bytes=VMEM_LIMIT),
        name="ffn_mix" if mix is not None else "ffn",
    )(*args)


_C_Z, _C_XBC, _C_Q, _C_K, _C_V, _C_DT, _C_END = 0, 512, 1280, 1792, 2304, 2816, 2944


def _head_norm(t, gain, ones_bd):
    ms = _split_dot(t * t, ones_bd, 2) * (1.0 / HEAD_DIM)
    return t * lax.rsqrt(ms + EPS) * gain


def _mixin_body(seq, tl, kv_transposed, x_ref, g_ref, w_ref, qn_ref, kn_ref, bd_ref, *refs):
    z_ref, xbc_ref, dt_ref, k_ref, v_ref, qa_ref, ka_ref, va_ref = refs[-8:]
    j = pl.program_id(1)
    h = _rms(x_ref[0], g_ref[...]).astype(BF16)
    p = _dot(h, w_ref[...])
    z_ref[0] = p[:, _C_Z:_C_XBC]
    xbc_ref[0] = p[:, _C_XBC:_C_Q]
    dt_ref[0] = p[:, _C_DT:_C_END]
    bd = bd_ref[...]
    q = _head_norm(p[:, _C_Q:_C_K], qn_ref[...], bd)
    k = _head_norm(p[:, _C_K:_C_V], kn_ref[...], bd)
    v = p[:, _C_V:_C_DT]
    if kv_transposed:
        k_t, v_t = k.T, v.T
        for s in range(k_ref.shape[0]):
            k_ref[s, 0] = k_t
            v_ref[s, 0] = v_t
    else:
        k_ref[0] = k
        v_ref[0] = v
    valid = (j * tl + _iota((tl, 1), 0)) < seq
    qa_ref[0] = jnp.where(valid, q * (HEAD_DIM ** -0.5), 0.0).astype(BF16)
    ka_ref[0] = jnp.where(valid, k, 0.0).astype(BF16)
    va_ref[0] = jnp.where(valid, v, 0.0).astype(BF16)


def _mixin(x, layer, norm, w_in, qn, kn, ones_bd, seq_pad, kv_stack=None):
    nb, seq, d = x.shape
    tl = _pick_tile(seq_pad, 768, 16 if kv_stack is None else LANES)
    blk = lambda b, j: (b, j, 0)
    lay3 = lambda b, j: (layer, 0, 0)
    f32_out = lambda c: jax.ShapeDtypeStruct((nb, seq, c), F32)
    att_out = jax.ShapeDtypeStruct((nb, seq_pad, WIDTH), BF16)
    args = [x, norm, w_in, qn, kn, ones_bd]
    in_specs = [pl.BlockSpec((1, tl, d), blk), _resident((None, 1, d), lay3),
                _resident((None, d, _C_END), lay3), _resident((None, 1, WIDTH), lay3),
                _resident((None, 1, WIDTH), lay3), _resident((WIDTH, WIDTH), lambda b, j: (0, 0))]
    aliases = {}
    if kv_stack is None:
        kv_spec = pl.BlockSpec((1, tl, WIDTH), blk)
        kv_out = f32_out(WIDTH)
    else:
        depth, k_stack, v_stack = kv_stack
        kv_out = jax.ShapeDtypeStruct((depth, nb, WIDTH, seq), F32)
        if k_stack is None:
            assert layer == 0
            kv_spec = pl.BlockSpec((depth, 1, WIDTH, tl), lambda b, j: (0, b, 0, j))
        else:
            kv_spec = pl.BlockSpec((1, 1, WIDTH, tl), lambda b, j: (layer, b, 0, j))
            aliases = {len(args): 3, len(args) + 1: 4}
            args += [k_stack, v_stack]
            in_specs += [pl.BlockSpec(memory_space=pl.ANY)] * 2
    return pl.pallas_call(
        functools.partial(_mixin_body, seq, tl, kv_stack is not None),
        grid=(nb, seq_pad // tl),
        in_specs=in_specs,
        out_specs=[pl.BlockSpec((1, tl, WIDTH), blk), pl.BlockSpec((1, tl, CONV_DIM), blk),
                   pl.BlockSpec((1, tl, DT_PAD), blk), kv_spec, kv_spec, pl.BlockSpec((1, tl, WIDTH), blk),
                   pl.BlockSpec((1, tl, WIDTH), blk), pl.BlockSpec((1, tl, WIDTH), blk)],
        out_shape=[f32_out(WIDTH), f32_out(CONV_DIM), f32_out(DT_PAD), kv_out, kv_out,
                   att_out, att_out, att_out],
        input_output_aliases=aliases,
        compiler_params=pltpu.CompilerParams(dimension_semantics=("parallel", "parallel"),
                                             vmem_limit_bytes=VMEM_LIMIT),
        name="mix_in",
    )(*args)


def _ssd_body(seq, q, xbc_ref, z_ref, dt_ref, cinit_ref, sinit_ref, cw_ref, cb_ref, dtb_ref, alog_ref,
              dsk_ref, nw_ref, exp_ref, y_ref, sout_ref, cbuf, st):
    c = pl.program_id(1)

    @pl.when(c == 0)
    def _():
        cbuf[0:8, :] = cinit_ref[0]
        st[...] = sinit_ref[0]

    valid = (c * q + _iota((q, 1), 0)) < seq
    xraw = jnp.where(valid, xbc_ref[0], 0.0)
    cbuf[8:8 + q, :] = xraw
    cw = cw_ref[...]
    conv = (cbuf[5:5 + q, :] * cw[0:1] + cbuf[6:6 + q, :] * cw[1:2]
            + cbuf[7:7 + q, :] * cw[2:3] + xraw * cw[3:4])
    cbuf[5:8, :] = cbuf[5 + q:8 + q, :]
    xc = _silu(conv + cb_ref[...])
    xs = xc[:, :WIDTH]
    bm = xc[:, WIDTH:WIDTH + LANES].astype(BF16)
    cm = xc[:, WIDTH + LANES:].astype(BF16)

    dt = jnp.where(valid, _softplus(dt_ref[0] + dtb_ref[...]), 0.0)
    a = dt * (-jnp.exp(alog_ref[...]))
    tri = jnp.where(_iota((q, q), 0) >= _iota((q, q), 1), 1.0, 0.0).astype(BF16)
    cs = None
    for part in _split(a, 3):
        term = _dot(tri, part)
        cs = term if cs is None else cs + term
    ecs = jnp.exp(cs)
    to_end = jnp.exp(cs[q - 1:q, :] - cs)
    expand = exp_ref[...]
    dt_e = _split_dot(dt, expand, 2)
    ecs_e = _split_dot(ecs, expand, 2)
    to_end_e = _split_dot(to_end, expand, 2)
    xdt = xs * dt_e

    lane = _iota((q, LANES), 1)
    low = lane < HEAD_DIM
    g0 = _dot_nt(jnp.where(low, cm, jnp.zeros_like(cm)), bm)
    g1 = _dot_nt(jnp.where(low, jnp.zeros_like(cm), cm), bm)
    cs_t = cs.T
    causal = _iota((q, q), 0) >= _iota((q, q), 1)
    xdt_b = xdt.astype(BF16)
    y_pairs = []
    for p in range(N_PAIRS):
        xp = xdt_b[:, p * LANES:(p + 1) * LANES]
        yp = None
        for h, xm in ((2 * p, jnp.where(low, xp, jnp.zeros_like(xp))),
                      (2 * p + 1, jnp.where(low, jnp.zeros_like(xp), xp))):
            seg = cs[:, h:h + 1] - cs_t[h:h + 1, :]
            decay = jnp.exp(jnp.where(causal, seg, -jnp.inf))
            m = ((g0 if h < N_HEADS // SSD_GROUPS else g1) * decay).astype(BF16)
            t = _dot(m, xm)
            yp = t if yp is None else yp + t
        y_pairs.append(yp)
    y_diag = jnp.concatenate(y_pairs, axis=1)

    s_prev = st[...]
    y_off = _dot(cm, s_prev.astype(BF16)) * ecs_e
    y = y_diag + y_off + dsk_ref[...] * xs
    zz = jnp.where(valid, z_ref[0], 0.0)
    y = y * _silu(zz)
    half = WIDTH // SSD_GROUPS
    y0, y1 = y[:, :half], y[:, half:]
    y0 = y0 * lax.rsqrt(jnp.mean(y0 * y0, axis=-1, keepdims=True) + EPS)
    y1 = y1 * lax.rsqrt(jnp.mean(y1 * y1, axis=-1, keepdims=True) + EPS)
    y_ref[0] = (jnp.concatenate([y0, y1], axis=1) * nw_ref[...]).astype(y_ref.dtype)

    upd = _dot_tn(bm, (xdt * to_end_e).astype(BF16))
    own = (_iota((LANES, WIDTH), 0) < HEAD_DIM) == (_iota((LANES, WIDTH), 1) < half)
    st[...] = s_prev * ecs_e[q - 1:q, :] + jnp.where(own, upd, 0.0)

    @pl.when(c == pl.num_programs(1) - 1)
    def _():
        sout_ref[0] = st[...]


def _ssd(xbc, z, dt, seq, conv_init, state_init, layer, cw, cb, dtb, alog, dsk, nw, expand, q=CHUNK):
    nb, n_rows, _ = xbc.shape
    nchunk = pl.cdiv(n_rows, q)
    blk = lambda b, c: (b, c, 0)
    per_b = lambda b, c: (b, 0, 0)
    lay3 = lambda b, c: (layer, 0, 0)
    return pl.pallas_call(
        functools.partial(_ssd_body, seq, q),
        grid=(nb, nchunk),
        in_specs=[pl.BlockSpec((1, q, CONV_DIM), blk), pl.BlockSpec((1, q, WIDTH), blk),
                  pl.BlockSpec((1, q, DT_PAD), blk), pl.BlockSpec((1, 8, CONV_DIM), per_b),
                  pl.BlockSpec((1, LANES, WIDTH), per_b), pl.BlockSpec((None, 8, CONV_DIM), lay3),
                  pl.BlockSpec((None, 1, CONV_DIM), lay3), pl.BlockSpec((None, 1, DT_PAD), lay3),
                  pl.BlockSpec((None, 1, DT_PAD), lay3), pl.BlockSpec((None, 1, WIDTH), lay3),
                  pl.BlockSpec((None, 1, WIDTH), lay3), pl.BlockSpec((LANES, WIDTH), lambda b, c: (0, 0))],
        out_specs=[pl.BlockSpec((1, q, WIDTH), blk), pl.BlockSpec((1, LANES, WIDTH), per_b)],
        out_shape=[jax.ShapeDtypeStruct((nb, n_rows, WIDTH), BF16),
                   jax.ShapeDtypeStruct((nb, LANES, WIDTH), F32)],
        scratch_shapes=[pltpu.VMEM((8 + q, CONV_DIM), F32), pltpu.VMEM((LANES, WIDTH), F32)],
        compiler_params=pltpu.CompilerParams(dimension_semantics=("parallel", "arbitrary"),
                                             vmem_limit_bytes=VMEM_LIMIT),
        name="ssd",
    )(xbc, z, dt, conv_init, state_init, cw, cb, dtb, alog, dsk, nw, expand)


def _state_to_blocks(s):
    nb = s.shape[0]
    hpg = N_HEADS // SSD_GROUPS
    t = jnp.transpose(s.reshape(nb, SSD_GROUPS, hpg, HEAD_DIM, HEAD_DIM), (0, 1, 4, 2, 3))
    t = t.reshape(nb, SSD_GROUPS, HEAD_DIM, hpg * HEAD_DIM)
    zero = jnp.zeros_like(t[:, 0])
    top = jnp.concatenate([t[:, 0], zero], axis=-1)
    bot = jnp.concatenate([zero, t[:, 1]], axis=-1)
    return jnp.concatenate([top, bot], axis=1)


def _blocks_to_state(sb):
    nb = sb.shape[0]
    hpg = N_HEADS // SSD_GROUPS
    half = WIDTH // SSD_GROUPS
    t = jnp.stack([sb[:, :HEAD_DIM, :half], sb[:, HEAD_DIM:, half:]], axis=1)
    t = t.reshape(nb, SSD_GROUPS, HEAD_DIM, hpg, HEAD_DIM)
    return jnp.transpose(t, (0, 1, 3, 4, 2)).reshape(nb, N_HEADS, HEAD_DIM, HEAD_DIM)


def _sb_softplus(z):
    return jnp.maximum(z, 0.0) + jnp.log(1.0 + jnp.exp2(jnp.abs(z) * (-LOG2E)))


def _suffix_matrix(tk):
    return jnp.where(_iota((tk, tk), 0) >= _iota((tk, tk), 1), 1.0, 0.0).astype(BF16)


def _attn_body(q_ref, k_ref, v_ref, bias_ref, g_ref, o_ref, qm_sc, z_sc, r_sc, acc_sc):
    i = pl.program_id(1)
    tq, tk = CHUNK, KEY_TILE
    low = _iota((tq, LANES), 1) < HEAD_DIM
    for p in range(N_PAIRS):
        qp = q_ref[0, :, p * LANES:(p + 1) * LANES]
        qm_sc[2 * p] = jnp.where(low, qp, jnp.zeros_like(qp))
        qm_sc[2 * p + 1] = jnp.where(low, jnp.zeros_like(qp), qp)
    kt_diag = i // 2
    heads = range(N_HEADS)
    pair = lambda h: slice((h // 2) * LANES, (h // 2 + 1) * LANES)
    u_full = _suffix_matrix(tk)
    u2_full = jnp.concatenate([u_full, u_full], axis=0)

    def suffix2(width):
        if width == tk:
            return u2_full
        u = _suffix_matrix(width)
        return jnp.concatenate([u, u], axis=0)

    def scores(start, width):
        ks = pl.multiple_of(start, width)
        return [_dot_nt(qm_sc[h], k_ref[0, pl.ds(ks, width), pair(h)]) + bias_ref[h] for h in heads]

    def tile(start, width, zs, first, visible, kt_next, next_slot):
        ks = pl.multiple_of(start, width)
        u2 = suffix2(width)
        ps = [_sb_softplus(z) for z in zs]
        if visible is not None:
            ps = [jnp.where(visible, p, 0.0) for p in ps]
        sufs = [_dot(jnp.concatenate(_split(p, 2), axis=1), u2) for p in ps]
        ws = [jnp.exp(z - s) for z, s in zip(zs, sufs)]
        if visible is not None:
            ws = [jnp.where(visible, w, 0.0) for w in ws]
        pvs = [_dot(ws[h].astype(BF16), v_ref[0, pl.ds(ks, width), pair(h)]) for h in heads]
        for h, z in zip(heads, scores(jnp.maximum(kt_next, 0) * tk, tk)):
            z_sc[next_slot, h] = z
        for h in heads:
            total = jnp.sum(ps[h], axis=1, keepdims=True)
            if first:
                acc_sc[h] = pvs[h]
                r_sc[h] = jnp.broadcast_to(total, (tq, LANES))
            else:
                later = r_sc[h]
                acc_sc[h] = acc_sc[h] + jnp.exp(-later) * pvs[h]
                r_sc[h] = later + total

    @pl.when(i % 2 == 0)
    def _():
        strictly_before = _iota((tq, tq), 1) < _iota((tq, tq), 0)
        tile(i * tq, tq, scores(i * tq, tq), True, strictly_before, kt_diag - 1, 0)

    @pl.when(i % 2 == 1)
    def _():
        before = _iota((tq, tk), 1) < _iota((tq, tk), 0) + tq
        tile(kt_diag * tk, tk, scores(kt_diag * tk, tk), True, before, kt_diag - 1, 0)

    def body(j, carry):
        slot = j % 2
        kt = kt_diag - 1 - j
        tile(kt * tk, tk, [z_sc[slot, h] for h in heads], False, None, kt - 1, 1 - slot)
        return carry

    lax.fori_loop(0, kt_diag, body, 0)
    outs = [jnp.where(low, acc_sc[2 * p], acc_sc[2 * p + 1]) for p in range(N_PAIRS)]
    o_ref[0] = _rms(jnp.concatenate(outs, axis=1), g_ref[...]).astype(o_ref.dtype)


def _attn(qa, ka, va, seq, layer, bias, out_norm):
    nb, seq_pad, _ = qa.shape
    assert seq_pad % KEY_TILE == 0
    nq = pl.cdiv(seq, CHUNK)
    head_tile = lambda: pltpu.VMEM((N_HEADS, CHUNK, LANES), F32)
    return pl.pallas_call(
        _attn_body,
        grid=(nb, nq),
        in_specs=[pl.BlockSpec((1, CHUNK, WIDTH), lambda b, i: (b, i, 0)),
                  pl.BlockSpec((1, seq_pad, WIDTH), lambda b, i: (b, 0, 0)),
                  pl.BlockSpec((1, seq_pad, WIDTH), lambda b, i: (b, 0, 0)),
                  pl.BlockSpec(memory_space=pltpu.SMEM),
                  pl.BlockSpec((None, 1, WIDTH), lambda b, i: (layer, 0, 0))],
        out_specs=pl.BlockSpec((1, CHUNK, WIDTH), lambda b, i: (b, i, 0)),
        out_shape=jax.ShapeDtypeStruct((nb, seq, WIDTH), BF16),
        scratch_shapes=[pltpu.VMEM((N_HEADS, CHUNK, LANES), BF16),
                        pltpu.VMEM((2, N_HEADS, CHUNK, KEY_TILE), F32), head_tile(), head_tile()],
        compiler_params=pltpu.CompilerParams(dimension_semantics=("parallel", "arbitrary"),
                                             vmem_limit_bytes=VMEM_LIMIT),
        name="sb_attn",
    )(qa, ka, va, bias, out_norm)


def _decode_body(n_group, pt_ref, q_ref, kn_ref, vn_ref, bias_ref, g_ref, *rest):
    k_refs = rest[:n_group]
    v_refs = rest[n_group:2 * n_group]
    o_ref, r_sc, acc_sc = rest[2 * n_group:]
    j = pl.program_id(1)
    t_new = q_ref.shape[1]
    rows = N_HEADS * t_new
    page = CHUNK
    u = _suffix_matrix(page)
    t_bits, d_bits = t_new.bit_length() - 1, HEAD_DIM.bit_length() - 1
    own = (_iota((rows, WIDTH), 0) >> t_bits) == (_iota((rows, WIDTH), 1) >> d_bits)
    q = q_ref[0]
    qbd = jnp.where(own, jnp.concatenate([q] * N_HEADS, axis=0), 0.0).astype(BF16)
    bias = bias_ref[...]

    @pl.when(j == 0)
    def _():
        pad = jnp.zeros((page - t_new, WIDTH), F32)
        kb = jnp.concatenate([kn_ref[0], pad], axis=0).astype(BF16)
        vb = jnp.concatenate([vn_ref[0], pad], axis=0).astype(BF16)
        z = _dot_nt(qbd, kb) + bias
        visible = _iota((rows, page), 1) < (_iota((rows, page), 0) & (t_new - 1))
        p = jnp.where(visible, _sb_softplus(z), 0.0)
        w = jnp.where(visible, jnp.exp(z - _split_dot(p, u, 2)), 0.0)
        acc_sc[...] = _dot(w.astype(BF16), vb)
        r_sc[...] = jnp.broadcast_to(jnp.sum(p, axis=1, keepdims=True), (rows, LANES))

    pages = range(n_group)
    zs = [_dot(qbd, k_refs[g][...].astype(BF16)) + bias for g in pages]
    ps = [_sb_softplus(z) for z in zs]
    u2 = jnp.concatenate([u, u], axis=0)
    sufs = [_dot(jnp.concatenate(_split(p, 2), axis=1), u2) for p in ps]
    later = r_sc[...]
    ws = []
    for g in pages:
        ws.append(jnp.exp(zs[g] - later - sufs[g]).astype(BF16))
        later = later + jnp.sum(ps[g], axis=1, keepdims=True)
    acc = acc_sc[...]
    for g in pages:
        acc = acc + _dot_nt(ws[g], v_refs[g][...].astype(BF16))
    acc_sc[...] = acc
    r_sc[...] = later

    @pl.when(j == pl.num_programs(1) - 1)
    def _():
        a = jnp.where(own, acc, 0.0)
        o = a[0:t_new]
        for h in range(1, N_HEADS):
            o = o + a[h * t_new:(h + 1) * t_new]
        o_ref[0] = _rms(o, g_ref[...]).astype(o_ref.dtype)


def _decode(q, k_new, v_new, cache_k, cache_v, page_table, layer, bias_tile, out_norm):
    db, t_new, _ = q.shape
    n_pages = page_table.shape[1]
    page = cache_k.shape[3]
    assert page == CHUNK and t_new == 8
    n_group = _pick_tile(n_pages, 32, 1)
    rows = N_HEADS * t_new
    per_b = lambda b, j, pt: (b, 0, 0)

    def page_spec(g):
        return pl.BlockSpec((None, None, WIDTH, page),
                            lambda b, j, pt: (layer, pt[b, n_pages - 1 - (j * n_group + g)], 0, 0))

    grid_spec = pltpu.PrefetchScalarGridSpec(
        num_scalar_prefetch=1,
        grid=(db, n_pages // n_group),
        in_specs=[pl.BlockSpec((1, t_new, WIDTH), per_b), pl.BlockSpec((1, t_new, WIDTH), per_b),
                  pl.BlockSpec((1, t_new, WIDTH), per_b),
                  pl.BlockSpec((None, rows, LANES), lambda b, j, pt: (layer, 0, 0)),
                  pl.BlockSpec((None, 1, WIDTH), lambda b, j, pt: (layer, 0, 0))]
                 + [page_spec(g) for g in range(n_group)] * 2,
        out_specs=pl.BlockSpec((1, t_new, WIDTH), per_b),
        scratch_shapes=[pltpu.VMEM((rows, LANES), F32), pltpu.VMEM((rows, WIDTH), F32)],
    )
    return pl.pallas_call(
        functools.partial(_decode_body, n_group),
        grid_spec=grid_spec,
        out_shape=jax.ShapeDtypeStruct((db, t_new, WIDTH), F32),
        compiler_params=pltpu.CompilerParams(dimension_semantics=("parallel", "arbitrary"),
                                             vmem_limit_bytes=VMEM_LIMIT),
        name="sb_decode",
    )(page_table, q, k_new, v_new, bias_tile, out_norm, *([cache_k] * n_group), *([cache_v] * n_group))


def kernel(x_prompt, x_sample, cache_k, cache_v, state_ssm, state_conv, page_table, meta_tokens,
           norm_ffn1, ffn1_w_gu, ffn1_w_down, norm_mix, w_in, conv_w, conv_b, dt_bias, A_log,
           D_skip, ssd_norm, q_norm, k_norm, sb_bias, sb_out_norm, w_out, norm_ffn2, ffn2_w_gu,
           ffn2_w_down):
    bp, seq_in, d = x_prompt.shape
    db, t_new, _ = x_sample.shape
    depth = w_in.shape[0]
    n_meta = meta_tokens.shape[0]
    seq = n_meta + seq_in
    seq_pad = pl.cdiv(seq, KEY_TILE) * KEY_TILE

    wgu1, wd1 = ffn1_w_gu.astype(BF16), ffn1_w_down.astype(BF16)
    wgu2, wd2 = ffn2_w_gu.astype(BF16), ffn2_w_down.astype(BF16)
    wo = w_out.astype(BF16)
    o_xbc, o_dt = WIDTH + CONV_DIM, WIDTH + CONV_DIM + N_HEADS
    w_dt = jnp.pad(w_in[:, :, o_xbc:o_dt], ((0, 0), (0, 0), (0, DT_PAD - N_HEADS)))
    w_in_r = jnp.concatenate([w_in[:, :, :o_xbc], w_in[:, :, o_dt:], w_dt], axis=-1).astype(BF16)
    assert w_in_r.shape[-1] == _C_END
    vec = lambda a: a[:, None, :]
    norm_ffn1, norm_mix, norm_ffn2 = vec(norm_ffn1), vec(norm_mix), vec(norm_ffn2)
    qn = vec(jnp.tile(q_norm, (1, N_HEADS)))
    kn = vec(jnp.tile(k_norm, (1, N_HEADS)))
    ones_bd = ((jnp.arange(WIDTH)[:, None] // HEAD_DIM) == (jnp.arange(WIDTH)[None, :] // HEAD_DIM)).astype(BF16)
    expand = (jnp.arange(LANES)[:, None] == (jnp.arange(WIDTH)[None, :] // HEAD_DIM)).astype(BF16)
    cw = jnp.pad(conv_w, ((0, 0), (0, 8 - CONV_W), (0, 0)))
    conv_b = vec(conv_b)
    dtb = vec(jnp.pad(dt_bias, ((0, 0), (0, DT_PAD - N_HEADS))))
    alog = vec(jnp.pad(A_log, ((0, 0), (0, DT_PAD - N_HEADS))))
    dsk = vec(jnp.repeat(D_skip, HEAD_DIM, axis=1))
    ssd_norm, sb_out_norm = vec(ssd_norm), vec(sb_out_norm)
    bias_tile = jnp.broadcast_to(jnp.repeat(sb_bias, t_new, axis=1)[:, :, None], (depth, N_HEADS * t_new, LANES))
    n_phys, page = cache_k.shape[1], cache_k.shape[2]
    page_t = lambda c: jnp.transpose(c, (0, 1, 3, 4, 2)).reshape(depth, n_phys, WIDTH, page)
    ck, cv = page_t(cache_k), page_t(cache_v)

    meta = jnp.broadcast_to(meta_tokens[None].astype(x_prompt.dtype), (bp, n_meta, d))
    xp = jnp.concatenate([meta, x_prompt], axis=1).reshape(bp * seq, d)
    xs = x_sample.reshape(db * t_new, d)
    conv0_p = jnp.zeros((bp, 8, CONV_DIM), F32)
    state0_p = jnp.zeros((bp, LANES, WIDTH), F32)

    outs = {name: [] for name in ("sp", "cp", "ks", "vs", "ss", "cs")}
    kp_stack = vp_stack = None
    for l in range(depth):
        xp = _ffn(xp, l, norm_ffn1, wgu1, wd1)
        z, xbc, dt, kp_stack, vp_stack, qa, ka, va = _mixin(xp.reshape(bp, seq, d), l, norm_mix, w_in_r, qn, kn,
                                                            ones_bd, seq_pad, kv_stack=(depth, kp_stack, vp_stack))
        y_ssd, s_fin = _ssd(xbc, z, dt, seq, conv0_p, state0_p, l, cw, conv_b, dtb, alog, dsk, ssd_norm, expand)
        y_sb = _attn(qa, ka, va, seq, l, sb_bias[l], sb_out_norm)
        xp = _ffn(xp, l, norm_ffn2, wgu2, wd2,
                  mix=(y_ssd.reshape(bp * seq, WIDTH), y_sb.reshape(bp * seq, WIDTH), wo))
        outs["sp"].append(_blocks_to_state(s_fin))
        outs["cp"].append(xbc[:, seq - (CONV_W - 1):, :])

        xs = _ffn(xs, l, norm_ffn1, wgu1, wd1)
        z, xbc, dt, k, v, qa, _, _ = _mixin(xs.reshape(1, db * t_new, d), l, norm_mix, w_in_r, qn, kn, ones_bd,
                                            db * t_new)
        shp = lambda a: a.reshape(db, t_new, a.shape[-1])
        xbc, k, v = shp(xbc), shp(k), shp(v)
        chunk = lambda a: jnp.pad(shp(a), ((0, 0), (0, SAMPLE_CHUNK - t_new), (0, 0)))
        conv0_s = jnp.pad(state_conv[l], ((0, 0), (8 - (CONV_W - 1), 0), (0, 0)))
        y_ssd, s_fin = _ssd(chunk(xbc), chunk(z), chunk(dt), t_new, conv0_s, _state_to_blocks(state_ssm[l]), l,
                            cw, conv_b, dtb, alog, dsk, ssd_norm, expand, q=SAMPLE_CHUNK)
        y_ssd = y_ssd[:, :t_new]
        y_sb = _decode(shp(qa).astype(F32), k, v, ck, cv, page_table, l, bias_tile, sb_out_norm)
        xs = _ffn(xs, l, norm_ffn2, wgu2, wd2,
                  mix=(y_ssd.reshape(db * t_new, WIDTH), y_sb.reshape(db * t_new, WIDTH), wo))
        outs["ks"].append(k.reshape(db, t_new, N_HEADS, HEAD_DIM))
        outs["vs"].append(v.reshape(db, t_new, N_HEADS, HEAD_DIM))
        outs["ss"].append(_blocks_to_state(s_fin))
        outs["cs"].append(jnp.concatenate([state_conv[l], xbc], axis=1)[:, -(CONV_W - 1):, :])

    y_prompt = xp.reshape(bp, seq, d)[:, n_meta:]
    y_sample = xs.reshape(db, t_new, d)
    st = lambda name: jnp.stack(outs[name])
    to_thd = lambda a: jnp.transpose(a.reshape(depth, bp, N_HEADS, HEAD_DIM, seq), (0, 1, 4, 2, 3))
    return (y_prompt, y_sample, to_thd(kp_stack), to_thd(vp_stack), st("sp"), st("cp"),
            st("ks"), st("vs"), st("ss"), st("cs"))
```

```python
import functools

import jax
import jax.numpy as jnp
from jax import lax
from jax.experimental import pallas as pl
from jax.experimental.pallas import tpu as pltpu

F32 = jnp.float32
BF16 = jnp.bfloat16
EPS = 1e-6
LOG2E = 1.4426950408889634

LANES = 128
HEAD_DIM = 64
N_HEADS = 8
WIDTH = N_HEADS * HEAD_DIM
N_PAIRS = WIDTH // LANES
SSD_GROUPS = 2
CONV_W = 4
CONV_DIM = WIDTH + 2 * SSD_GROUPS * HEAD_DIM
DT_PAD = LANES
CHUNK = 128
SAMPLE_CHUNK = 16
KEY_TILE = 256
VMEM_LIMIT = 56 * 1024 * 1024


def _dot(a, b):
    return jnp.dot(a, b, preferred_element_type=F32)


def _dot_nt(a, b):
    return lax.dot_general(a, b, (((1,), (1,)), ((), ())), preferred_element_type=F32)


def _dot_tn(a, b):
    return lax.dot_general(a, b, (((0,), (0,)), ((), ())), preferred_element_type=F32)


def _split(a, passes):
    parts = []
    rem = a
    for p in range(passes):
        part = rem.astype(BF16)
        parts.append(part)
        if p + 1 < passes:
            rem = rem - part.astype(F32)
    return parts


def _split_dot(a, m, passes):
    out = None
    for part in _split(a, passes):
        term = _dot(part, m)
        out = term if out is None else out + term
    return out


def _rms(x, g):
    ms = jnp.mean(x * x, axis=-1, keepdims=True)
    return x * lax.rsqrt(ms + EPS) * g


def _silu(x):
    return x * jax.nn.sigmoid(x)


def _softplus(x):
    return jnp.maximum(x, 0.0) + jnp.log1p(jnp.exp(-jnp.abs(x)))


def _iota(shape, dim):
    return lax.broadcasted_iota(jnp.int32, shape, dim)


def _pick_tile(n, cap, mult):
    if n <= cap:
        return n
    best = None
    for t in range(mult, cap + 1, mult):
        if n % t == 0:
            best = t
    assert best is not None, (n, cap, mult)
    return best


def _resident(shape, index_map):
    return pl.BlockSpec(shape, index_map, pipeline_mode=pl.Buffered(1))


def _ffn_body(has_mix, n_chunks, fc, *refs):
    if has_mix:
        x_ref, ys_ref, yb_ref, wo_ref, g_ref, wgu_ref, wd_ref, o_ref = refs
        x = (x_ref[...]
             + _dot(ys_ref[...].astype(BF16), wo_ref[:WIDTH, :])
             + _dot(yb_ref[...].astype(BF16), wo_ref[WIDTH:, :]))
    else:
        x_ref, g_ref, wgu_ref, wd_ref, o_ref = refs
        x = x_ref[...]
    h = _rms(x, g_ref[...]).astype(BF16)
    d_ff = n_chunks * fc
    acc = None
    for c in range(n_chunks):
        gate = _dot(h, wgu_ref[:, c * fc:(c + 1) * fc])
        up = _dot(h, wgu_ref[:, d_ff + c * fc:d_ff + (c + 1) * fc])
        a = (_silu(gate) * up).astype(BF16)
        d = _dot(a, wd_ref[c * fc:(c + 1) * fc, :])
        acc = d if acc is None else acc + d
    o_ref[...] = x + 0.5 * acc


def _ffn(x, layer, norm, wgu, wd, mix=None):
    t, d = x.shape
    d_ff = wd.shape[1]
    fc = 256
    assert d_ff % fc == 0
    tm = _pick_tile(t, 704, 16)
    row = lambda i: (i, 0)
    lay3 = lambda i: (layer, 0, 0)
    in_specs = [pl.BlockSpec((tm, d), row)]
    args = [x]
    if mix is not None:
        ys, yb, wo = mix
        in_specs += [pl.BlockSpec((tm, WIDTH), row), pl.BlockSpec((tm, WIDTH), row),
                     _resident((None, 2 * WIDTH, d), lay3)]
        args += [ys, yb, wo]
    in_specs += [_resident((None, 1, d), lay3), _resident((None, d, 2 * d_ff), lay3),
                 _resident((None, d_ff, d), lay3)]
    args += [norm, wgu, wd]
    return pl.pallas_call(
        functools.partial(_ffn_body, mix is not None, d_ff // fc, fc),
        grid=(t // tm,),
        in_specs=in_specs,
        out_specs=pl.BlockSpec((tm, d), row),
        out_shape=jax.ShapeDtypeStruct((t, d), F32),
        compiler_params=pltpu.CompilerParams(dimension_semantics=("parallel",), vmem_limit_bytes=VMEM_LIMIT),
        name="ffn_mix" if mix is not None else "ffn",
    )(*args)


_C_Z, _C_XBC, _C_Q, _C_K, _C_V, _C_DT, _C_END = 0, 512, 1280, 1792, 2304, 2816, 2944


def _head_norm(t, gain, ones_bd):
    ms = _split_dot(t * t, ones_bd, 2) * (1.0 / HEAD_DIM)
    return t * lax.rsqrt(ms + EPS) * gain


def _mixin_body(seq, tl, kv_transposed, x_ref, g_ref, w_ref, qn_ref, kn_ref, bd_ref, *refs):
    z_ref, xbc_ref, dt_ref, k_ref, v_ref, qa_ref, ka_ref, va_ref = refs[-8:]
    j = pl.program_id(1)
    h = _rms(x_ref[0], g_ref[...]).astype(BF16)
    p = _dot(h, w_ref[...])
    z_ref[0] = p[:, _C_Z:_C_XBC]
    xbc_ref[0] = p[:, _C_XBC:_C_Q]
    dt_ref[0] = p[:, _C_DT:_C_END]
    bd = bd_ref[...]
    q = _head_norm(p[:, _C_Q:_C_K], qn_ref[...], bd)
    k = _head_norm(p[:, _C_K:_C_V], kn_ref[...], bd)
    v = p[:, _C_V:_C_DT]
    if kv_transposed:
        k_t, v_t = k.T, v.T
        for s in range(k_ref.shape[0]):
            k_ref[s, 0] = k_t
            v_ref[s, 0] = v_t
    else:
        k_ref[0] = k
        v_ref[0] = v
    valid = (j * tl + _iota((tl, 1), 0)) < seq
    qa_ref[0] = jnp.where(valid, q * (HEAD_DIM ** -0.5), 0.0).astype(BF16)
    ka_ref[0] = jnp.where(valid, k, 0.0).astype(BF16)
    va_ref[0] = jnp.where(valid, v, 0.0).astype(BF16)


def _mixin(x, layer, norm, w_in, qn, kn, ones_bd, seq_pad, kv_stack=None):
    nb, seq, d = x.shape
    tl = _pick_tile(seq_pad, 768, 16 if kv_stack is None else LANES)
    blk = lambda b, j: (b, j, 0)
    lay3 = lambda b, j: (layer, 0, 0)
    f32_out = lambda c: jax.ShapeDtypeStruct((nb, seq, c), F32)
    att_out = jax.ShapeDtypeStruct((nb, seq_pad, WIDTH), BF16)
    args = [x, norm, w_in, qn, kn, ones_bd]
    in_specs = [pl.BlockSpec((1, tl, d), blk), _resident((None, 1, d), lay3),
                _resident((None, d, _C_END), lay3), _resident((None, 1, WIDTH), lay3),
                _resident((None, 1, WIDTH), lay3), _resident((WIDTH, WIDTH), lambda b, j: (0, 0))]
    aliases = {}
    if kv_stack is None:
        kv_spec = pl.BlockSpec((1, tl, WIDTH), blk)
        kv_out = f32_out(WIDTH)
    else:
        depth, k_stack, v_stack = kv_stack
        kv_out = jax.ShapeDtypeStruct((depth, nb, WIDTH, seq), F32)
        if k_stack is None:
            assert layer == 0
            kv_spec = pl.BlockSpec((depth, 1, WIDTH, tl), lambda b, j: (0, b, 0, j))
        else:
            kv_spec = pl.BlockSpec((1, 1, WIDTH, tl), lambda b, j: (layer, b, 0, j))
            aliases = {len(args): 3, len(args) + 1: 4}
            args += [k_stack, v_stack]
            in_specs += [pl.BlockSpec(memory_space=pl.ANY)] * 2
    return pl.pallas_call(
        functools.partial(_mixin_body, seq, tl, kv_stack is not None),
        grid=(nb, seq_pad // tl),
        in_specs=in_specs,
        out_specs=[pl.BlockSpec((1, tl, WIDTH), blk), pl.BlockSpec((1, tl, CONV_DIM), blk),
                   pl.BlockSpec((1, tl, DT_PAD), blk), kv_spec, kv_spec, pl.BlockSpec((1, tl, WIDTH), blk),
                   pl.BlockSpec((1, tl, WIDTH), blk), pl.BlockSpec((1, tl, WIDTH), blk)],
        out_shape=[f32_out(WIDTH), f32_out(CONV_DIM), f32_out(DT_PAD), kv_out, kv_out,
                   att_out, att_out, att_out],
        input_output_aliases=aliases,
        compiler_params=pltpu.CompilerParams(dimension_semantics=("parallel", "parallel"),
                                             vmem_limit_bytes=VMEM_LIMIT),
        name="mix_in",
    )(*args)


def _ssd_body(seq, q, xbc_ref, z_ref, dt_ref, cinit_ref, sinit_ref, cw_ref, cb_ref, dtb_ref, alog_ref,
              dsk_ref, nw_ref, exp_ref, y_ref, sout_ref, cbuf, st):
    c = pl.program_id(1)

    @pl.when(c == 0)
    def _():
        cbuf[0:8, :] = cinit_ref[0]
        st[...] = sinit_ref[0]

    valid = (c * q + _iota((q, 1), 0)) < seq
    xraw = jnp.where(valid, xbc_ref[0], 0.0)
    cbuf[8:8 + q, :] = xraw
    cw = cw_ref[...]
    conv = (cbuf[5:5 + q, :] * cw[0:1] + cbuf[6:6 + q, :] * cw[1:2]
            + cbuf[7:7 + q, :] * cw[2:3] + xraw * cw[3:4])
    cbuf[5:8, :] = cbuf[5 + q:8 + q, :]
    xc = _silu(conv + cb_ref[...])
    xs = xc[:, :WIDTH]
    bm = xc[:, WIDTH:WIDTH + LANES].astype(BF16)
    cm = xc[:, WIDTH + LANES:].astype(BF16)

    dt = jnp.where(valid, _softplus(dt_ref[0] + dtb_ref[...]), 0.0)
    a = dt * (-jnp.exp(alog_ref[...]))
    tri = jnp.where(_iota((q, q), 0) >= _iota((q, q), 1), 1.0, 0.0).astype(BF16)
    cs = None
    for part in _split(a, 3):
        term = _dot(tri, part)
        cs = term if cs is None else cs + term
    ecs = jnp.exp(cs)
    to_end = jnp.exp(cs[q - 1:q, :] - cs)
    expand = exp_ref[...]
    dt_e = _split_dot(dt, expand, 2)
    ecs_e = _split_dot(ecs, expand, 2)
    to_end_e = _split_dot(to_end, expand, 2)
    xdt = xs * dt_e

    lane = _iota((q, LANES), 1)
    low = lane < HEAD_DIM
    g0 = _dot_nt(jnp.where(low, cm, jnp.zeros_like(cm)), bm)
    g1 = _dot_nt(jnp.where(low, jnp.zeros_like(cm), cm), bm)
    cs_t = cs.T
    causal = _iota((q, q), 0) >= _iota((q, q), 1)
    xdt_b = xdt.astype(BF16)
    y_pairs = []
    for p in range(N_PAIRS):
        xp = xdt_b[:, p * LANES:(p + 1) * LANES]
        yp = None
        for h, xm in ((2 * p, jnp.where(low, xp, jnp.zeros_like(xp))),
                      (2 * p + 1, jnp.where(low, jnp.zeros_like(xp), xp))):
            seg = cs[:, h:h + 1] - cs_t[h:h + 1, :]
            decay = jnp.exp(jnp.where(causal, seg, -jnp.inf))
            m = ((g0 if h < N_HEADS // SSD_GROUPS else g1) * decay).astype(BF16)
            t = _dot(m, xm)
            yp = t if yp is None else yp + t
        y_pairs.append(yp)
    y_diag = jnp.concatenate(y_pairs, axis=1)

    s_prev = st[...]
    y_off = _dot(cm, s_prev.astype(BF16)) * ecs_e
    y = y_diag + y_off + dsk_ref[...] * xs
    zz = jnp.where(valid, z_ref[0], 0.0)
    y = y * _silu(zz)
    half = WIDTH // SSD_GROUPS
    y0, y1 = y[:, :half], y[:, half:]
    y0 = y0 * lax.rsqrt(jnp.mean(y0 * y0, axis=-1, keepdims=True) + EPS)
    y1 = y1 * lax.rsqrt(jnp.mean(y1 * y1, axis=-1, keepdims=True) + EPS)
    y_ref[0] = (jnp.concatenate([y0, y1], axis=1) * nw_ref[...]).astype(y_ref.dtype)

    upd = _dot_tn(bm, (xdt * to_end_e).astype(BF16))
    own = (_iota((LANES, WIDTH), 0) < HEAD_DIM) == (_iota((LANES, WIDTH), 1) < half)
    st[...] = s_prev * ecs_e[q - 1:q, :] + jnp.where(own, upd, 0.0)

    @pl.when(c == pl.num_programs(1) - 1)
    def _():
        sout_ref[0] = st[...]


def _ssd(xbc, z, dt, seq, conv_init, state_init, layer, cw, cb, dtb, alog, dsk, nw, expand, q=CHUNK):
    nb, n_rows, _ = xbc.shape
    nchunk = pl.cdiv(n_rows, q)
    blk = lambda b, c: (b, c, 0)
    per_b = lambda b, c: (b, 0, 0)
    lay3 = lambda b, c: (layer, 0, 0)
    return pl.pallas_call(
        functools.partial(_ssd_body, seq, q),
        grid=(nb, nchunk),
        in_specs=[pl.BlockSpec((1, q, CONV_DIM), blk), pl.BlockSpec((1, q, WIDTH), blk),
                  pl.BlockSpec((1, q, DT_PAD), blk), pl.BlockSpec((1, 8, CONV_DIM), per_b),
                  pl.BlockSpec((1, LANES, WIDTH), per_b), pl.BlockSpec((None, 8, CONV_DIM), lay3),
                  pl.BlockSpec((None, 1, CONV_DIM), lay3), pl.BlockSpec((None, 1, DT_PAD), lay3),
                  pl.BlockSpec((None, 1, DT_PAD), lay3), pl.BlockSpec((None, 1, WIDTH), lay3),
                  pl.BlockSpec((None, 1, WIDTH), lay3), pl.BlockSpec((LANES, WIDTH), lambda b, c: (0, 0))],
        out_specs=[pl.BlockSpec((1, q, WIDTH), blk), pl.BlockSpec((1, LANES, WIDTH), per_b)],
        out_shape=[jax.ShapeDtypeStruct((nb, n_rows, WIDTH), BF16),
                   jax.ShapeDtypeStruct((nb, LANES, WIDTH), F32)],
        scratch_shapes=[pltpu.VMEM((8 + q, CONV_DIM), F32), pltpu.VMEM((LANES, WIDTH), F32)],
        compiler_params=pltpu.CompilerParams(dimension_semantics=("parallel", "arbitrary"),
                                             vmem_limit_bytes=VMEM_LIMIT),
        name="ssd",
    )(xbc, z, dt, conv_init, state_init, cw, cb, dtb, alog, dsk, nw, expand)


def _state_to_blocks(s):
    nb = s.shape[0]
    hpg = N_HEADS // SSD_GROUPS
    t = jnp.transpose(s.reshape(nb, SSD_GROUPS, hpg, HEAD_DIM, HEAD_DIM), (0, 1, 4, 2, 3))
    t = t.reshape(nb, SSD_GROUPS, HEAD_DIM, hpg * HEAD_DIM)
    zero = jnp.zeros_like(t[:, 0])
    top = jnp.concatenate([t[:, 0], zero], axis=-1)
    bot = jnp.concatenate([zero, t[:, 1]], axis=-1)
    return jnp.concatenate([top, bot], axis=1)


def _blocks_to_state(sb):
    nb = sb.shape[0]
    hpg = N_HEADS // SSD_GROUPS
    half = WIDTH // SSD_GROUPS
    t = jnp.stack([sb[:, :HEAD_DIM, :half], sb[:, HEAD_DIM:, half:]], axis=1)
    t = t.reshape(nb, SSD_GROUPS, HEAD_DIM, hpg, HEAD_DIM)
    return jnp.transpose(t, (0, 1, 3, 4, 2)).reshape(nb, N_HEADS, HEAD_DIM, HEAD_DIM)


def _sb_softplus(z):
    return jnp.maximum(z, 0.0) + jnp.log(1.0 + jnp.exp2(jnp.abs(z) * (-LOG2E)))


def _suffix_matrix(tk):
    return jnp.where(_iota((tk, tk), 0) >= _iota((tk, tk), 1), 1.0, 0.0).astype(BF16)


def _attn_body(q_ref, k_ref, v_ref, bias_ref, g_ref, o_ref, qm_sc, z_sc, r_sc, acc_sc):
    i = pl.program_id(1)
    tq, tk = CHUNK, KEY_TILE
    low = _iota((tq, LANES), 1) < HEAD_DIM
    for p in range(N_PAIRS):
        qp = q_ref[0, :, p * LANES:(p + 1) * LANES]
        qm_sc[2 * p] = jnp.where(low, qp, jnp.zeros_like(qp))
        qm_sc[2 * p + 1] = jnp.where(low, jnp.zeros_like(qp), qp)
    kt_diag = i // 2
    heads = range(N_HEADS)
    pair = lambda h: slice((h // 2) * LANES, (h // 2 + 1) * LANES)
    u_full = _suffix_matrix(tk)
    u2_full = jnp.concatenate([u_full, u_full], axis=0)

    def suffix2(width):
        if width == tk:
            return u2_full
        u = _suffix_matrix(width)
        return jnp.concatenate([u, u], axis=0)

    def scores(start, width):
        ks = pl.multiple_of(start, width)
        return [_dot_nt(qm_sc[h], k_ref[0, pl.ds(ks, width), pair(h)]) + bias_ref[h] for h in heads]

    def tile(start, width, zs, first, visible, kt_next, next_slot):
        ks = pl.multiple_of(start, width)
        u2 = suffix2(width)
        ps = [_sb_softplus(z) for z in zs]
        if visible is not None:
            ps = [jnp.where(visible, p, 0.0) for p in ps]
        sufs = [_dot(jnp.concatenate(_split(p, 2), axis=1), u2) for p in ps]
        ws = [jnp.exp(z - s) for z, s in zip(zs, sufs)]
        if visible is not None:
            ws = [jnp.where(visible, w, 0.0) for w in ws]
        pvs = [_dot(ws[h].astype(BF16), v_ref[0, pl.ds(ks, width), pair(h)]) for h in heads]
        if kt_next is not None:
            for h, z in zip(heads, scores(jnp.maximum(kt_next, 0) * tk, tk)):
                z_sc[next_slot, h] = z
        for h in heads:
            total = jnp.sum(ps[h], axis=1, keepdims=True)
            if first:
                acc_sc[h] = pvs[h]
                r_sc[h] = jnp.broadcast_to(total, (tq, LANES))
            else:
                later = r_sc[h]
                acc_sc[h] = acc_sc[h] + jnp.exp(-later) * pvs[h]
                r_sc[h] = later + total

    @pl.when(i % 2 == 0)
    def _():
        strictly_before = _iota((tq, tq), 1) < _iota((tq, tq), 0)
        tile(i * tq, tq, scores(i * tq, tq), True, strictly_before, kt_diag - 1, 0)

    @pl.when(i % 2 == 1)
    def _():
        before = _iota((tq, tk), 1) < _iota((tq, tk), 0) + tq
        tile(kt_diag * tk, tk, scores(kt_diag * tk, tk), True, before, kt_diag - 1, 0)

    def body(j, carry):
        slot = j % 2
        kt = kt_diag - 1 - j
        tile(kt * tk, tk, [z_sc[slot, h] for h in heads], False, None, kt - 1, 1 - slot)
        return carry

    lax.fori_loop(0, kt_diag - 1, body, 0)

    @pl.when(kt_diag > 0)
    def _():
        slot = (kt_diag - 1) % 2
        tile(0 * kt_diag, tk, [z_sc[slot, h] for h in heads], False, None, None, 0)

    outs = [jnp.where(low, acc_sc[2 * p], acc_sc[2 * p + 1]) for p in range(N_PAIRS)]
    o_ref[0] = _rms(jnp.concatenate(outs, axis=1), g_ref[...]).astype(o_ref.dtype)


def _attn(qa, ka, va, seq, layer, bias, out_norm):
    nb, seq_pad, _ = qa.shape
    assert seq_pad % KEY_TILE == 0
    nq = pl.cdiv(seq, CHUNK)
    head_tile = lambda: pltpu.VMEM((N_HEADS, CHUNK, LANES), F32)
    return pl.pallas_call(
        _attn_body,
        grid=(nb, nq),
        in_specs=[pl.BlockSpec((1, CHUNK, WIDTH), lambda b, i: (b, i, 0)),
                  pl.BlockSpec((1, seq_pad, WIDTH), lambda b, i: (b, 0, 0)),
                  pl.BlockSpec((1, seq_pad, WIDTH), lambda b, i: (b, 0, 0)),
                  pl.BlockSpec(memory_space=pltpu.SMEM),
                  pl.BlockSpec((None, 1, WIDTH), lambda b, i: (layer, 0, 0))],
        out_specs=pl.BlockSpec((1, CHUNK, WIDTH), lambda b, i: (b, i, 0)),
        out_shape=jax.ShapeDtypeStruct((nb, seq, WIDTH), BF16),
        scratch_shapes=[pltpu.VMEM((N_HEADS, CHUNK, LANES), BF16),
                        pltpu.VMEM((2, N_HEADS, CHUNK, KEY_TILE), F32), head_tile(), head_tile()],
        compiler_params=pltpu.CompilerParams(dimension_semantics=("parallel", "arbitrary"),
                                             vmem_limit_bytes=VMEM_LIMIT),
        name="sb_attn",
    )(qa, ka, va, bias, out_norm)


def _decode_body(n_group, pt_ref, q_ref, kn_ref, vn_ref, bias_ref, g_ref, *rest):
    k_refs = rest[:n_group]
    v_refs = rest[n_group:2 * n_group]
    o_ref, r_sc, acc_sc = rest[2 * n_group:]
    j = pl.program_id(1)
    t_new = q_ref.shape[1]
    rows = N_HEADS * t_new
    page = CHUNK
    u = _suffix_matrix(page)
    t_bits, d_bits = t_new.bit_length() - 1, HEAD_DIM.bit_length() - 1
    own = (_iota((rows, WIDTH), 0) >> t_bits) == (_iota((rows, WIDTH), 1) >> d_bits)
    q = q_ref[0]
    qbd = jnp.where(own, jnp.concatenate([q] * N_HEADS, axis=0), 0.0).astype(BF16)
    bias = bias_ref[...]

    @pl.when(j == 0)
    def _():
        pad = jnp.zeros((page - t_new, WIDTH), F32)
        kb = jnp.concatenate([kn_ref[0], pad], axis=0).astype(BF16)
        vb = jnp.concatenate([vn_ref[0], pad], axis=0).astype(BF16)
        z = _dot_nt(qbd, kb) + bias
        visible = _iota((rows, page), 1) < (_iota((rows, page), 0) & (t_new - 1))
        p = jnp.where(visible, _sb_softplus(z), 0.0)
        w = jnp.where(visible, jnp.exp(z - _split_dot(p, u, 2)), 0.0)
        acc_sc[...] = _dot(w.astype(BF16), vb)
        r_sc[...] = jnp.broadcast_to(jnp.sum(p, axis=1, keepdims=True), (rows, LANES))

    pages = range(n_group)
    zs = [_dot(qbd, k_refs[g][...].astype(BF16)) + bias for g in pages]
    ps = [_sb_softplus(z) for z in zs]
    u2 = jnp.concatenate([u, u], axis=0)
    sufs = [_dot(jnp.concatenate(_split(p, 2), axis=1), u2) for p in ps]
    later = r_sc[...]
    ws = []
    for g in pages:
        ws.append(jnp.exp(zs[g] - later - sufs[g]).astype(BF16))
        later = later + jnp.sum(ps[g], axis=1, keepdims=True)
    acc = acc_sc[...]
    for g in pages:
        acc = acc + _dot_nt(ws[g], v_refs[g][...].astype(BF16))
    acc_sc[...] = acc
    r_sc[...] = later

    @pl.when(j == pl.num_programs(1) - 1)
    def _():
        a = jnp.where(own, acc, 0.0)
        o = a[0:t_new]
        for h in range(1, N_HEADS):
            o = o + a[h * t_new:(h + 1) * t_new]
        o_ref[0] = _rms(o, g_ref[...]).astype(o_ref.dtype)


def _decode(q, k_new, v_new, cache_k, cache_v, page_table, layer, bias_tile, out_norm):
    db, t_new, _ = q.shape
    n_pages = page_table.shape[1]
    page = cache_k.shape[3]
    assert page == CHUNK and t_new == 8
    n_group = _pick_tile(n_pages, 32, 1)
    rows = N_HEADS * t_new
    per_b = lambda b, j, pt: (b, 0, 0)

    def page_spec(g):
        return pl.BlockSpec((None, None, WIDTH, page),
                            lambda b, j, pt: (layer, pt[b, n_pages - 1 - (j * n_group + g)], 0, 0))

    grid_spec = pltpu.PrefetchScalarGridSpec(
        num_scalar_prefetch=1,
        grid=(db, n_pages // n_group),
        in_specs=[pl.BlockSpec((1, t_new, WIDTH), per_b), pl.BlockSpec((1, t_new, WIDTH), per_b),
                  pl.BlockSpec((1, t_new, WIDTH), per_b),
                  pl.BlockSpec((None, rows, LANES), lambda b, j, pt: (layer, 0, 0)),
                  pl.BlockSpec((None, 1, WIDTH), lambda b, j, pt: (layer, 0, 0))]
                 + [page_spec(g) for g in range(n_group)] * 2,
        out_specs=pl.BlockSpec((1, t_new, WIDTH), per_b),
        scratch_shapes=[pltpu.VMEM((rows, LANES), F32), pltpu.VMEM((rows, WIDTH), F32)],
    )
    return pl.pallas_call(
        functools.partial(_decode_body, n_group),
        grid_spec=grid_spec,
        out_shape=jax.ShapeDtypeStruct((db, t_new, WIDTH), F32),
        compiler_params=pltpu.CompilerParams(dimension_semantics=("parallel", "arbitrary"),
                                             vmem_limit_bytes=VMEM_LIMIT),
        name="sb_decode",
    )(page_table, q, k_new, v_new, bias_tile, out_norm, *([cache_k] * n_group), *([cache_v] * n_group))


def kernel(x_prompt, x_sample, cache_k, cache_v, state_ssm, state_conv, page_table, meta_tokens,
           norm_ffn1, ffn1_w_gu, ffn1_w_down, norm_mix, w_in, conv_w, conv_b, dt_bias, A_log,
           D_skip, ssd_norm, q_norm, k_norm, sb_bias, sb_out_norm, w_out, norm_ffn2, ffn2_w_gu,
           ffn2_w_down):
    bp, seq_in, d = x_prompt.shape
    db, t_new, _ = x_sample.shape
    depth = w_in.shape[0]
    n_meta = meta_tokens.shape[0]
    seq = n_meta + seq_in
    seq_pad = pl.cdiv(seq, KEY_TILE) * KEY_TILE

    wgu1, wd1 = ffn1_w_gu.astype(BF16), ffn1_w_down.astype(BF16)
    wgu2, wd2 = ffn2_w_gu.astype(BF16), ffn2_w_down.astype(BF16)
    wo = w_out.astype(BF16)
    o_xbc, o_dt = WIDTH + CONV_DIM, WIDTH + CONV_DIM + N_HEADS
    w_dt = jnp.pad(w_in[:, :, o_xbc:o_dt], ((0, 0), (0, 0), (0, DT_PAD - N_HEADS)))
    w_in_r = jnp.concatenate([w_in[:, :, :o_xbc], w_in[:, :, o_dt:], w_dt], axis=-1).astype(BF16)
    assert w_in_r.shape[-1] == _C_END
    vec = lambda a: a[:, None, :]
    norm_ffn1, norm_mix, norm_ffn2 = vec(norm_ffn1), vec(norm_mix), vec(norm_ffn2)
    qn = vec(jnp.tile(q_norm, (1, N_HEADS)))
    kn = vec(jnp.tile(k_norm, (1, N_HEADS)))
    ones_bd = ((jnp.arange(WIDTH)[:, None] // HEAD_DIM) == (jnp.arange(WIDTH)[None, :] // HEAD_DIM)).astype(BF16)
    expand = (jnp.arange(LANES)[:, None] == (jnp.arange(WIDTH)[None, :] // HEAD_DIM)).astype(BF16)
    cw = jnp.pad(conv_w, ((0, 0), (0, 8 - CONV_W), (0, 0)))
    conv_b = vec(conv_b)
    dtb = vec(jnp.pad(dt_bias, ((0, 0), (0, DT_PAD - N_HEADS))))
    alog = vec(jnp.pad(A_log, ((0, 0), (0, DT_PAD - N_HEADS))))
    dsk = vec(jnp.repeat(D_skip, HEAD_DIM, axis=1))
    ssd_norm, sb_out_norm = vec(ssd_norm), vec(sb_out_norm)
    bias_tile = jnp.broadcast_to(jnp.repeat(sb_bias, t_new, axis=1)[:, :, None], (depth, N_HEADS * t_new, LANES))
    n_phys, page = cache_k.shape[1], cache_k.shape[2]
    page_t = lambda c: jnp.transpose(c, (0, 1, 3, 4, 2)).reshape(depth, n_phys, WIDTH, page)
    ck, cv = page_t(cache_k), page_t(cache_v)

    meta = jnp.broadcast_to(meta_tokens[None].astype(x_prompt.dtype), (bp, n_meta, d))
    xp = jnp.concatenate([meta, x_prompt], axis=1).reshape(bp * seq, d)
    xs = x_sample.reshape(db * t_new, d)
    conv0_p = jnp.zeros((bp, 8, CONV_DIM), F32)
    state0_p = jnp.zeros((bp, LANES, WIDTH), F32)

    outs = {name: [] for name in ("sp", "cp", "ks", "vs", "ss", "cs")}
    kp_stack = vp_stack = None
    for l in range(depth):
        xp = _ffn(xp, l, norm_ffn1, wgu1, wd1)
        z, xbc, dt, kp_stack, vp_stack, qa, ka, va = _mixin(xp.reshape(bp, seq, d), l, norm_mix, w_in_r, qn, kn,
                                                            ones_bd, seq_pad, kv_stack=(depth, kp_stack, vp_stack))
        y_ssd, s_fin = _ssd(xbc, z, dt, seq, conv0_p, state0_p, l, cw, conv_b, dtb, alog, dsk, ssd_norm, expand)
        y_sb = _attn(qa, ka, va, seq, l, sb_bias[l], sb_out_norm)
        xp = _ffn(xp, l, norm_ffn2, wgu2, wd2,
                  mix=(y_ssd.reshape(bp * seq, WIDTH), y_sb.reshape(bp * seq, WIDTH), wo))
        outs["sp"].append(_blocks_to_state(s_fin))
        outs["cp"].append(xbc[:, seq - (CONV_W - 1):, :])

        xs = _ffn(xs, l, norm_ffn1, wgu1, wd1)
        z, xbc, dt, k, v, qa, _, _ = _mixin(xs.reshape(1, db * t_new, d), l, norm_mix, w_in_r, qn, kn, ones_bd,
                                            db * t_new)
        shp = lambda a: a.reshape(db, t_new, a.shape[-1])
        xbc, k, v = shp(xbc), shp(k), shp(v)
        chunk = lambda a: jnp.pad(shp(a), ((0, 0), (0, SAMPLE_CHUNK - t_new), (0, 0)))
        conv0_s = jnp.pad(state_conv[l], ((0, 0), (8 - (CONV_W - 1), 0), (0, 0)))
        y_ssd, s_fin = _ssd(chunk(xbc), chunk(z), chunk(dt), t_new, conv0_s, _state_to_blocks(state_ssm[l]), l,
                            cw, conv_b, dtb, alog, dsk, ssd_norm, expand, q=SAMPLE_CHUNK)
        y_ssd = y_ssd[:, :t_new]
        y_sb = _decode(shp(qa).astype(F32), k, v, ck, cv, page_table, l, bias_tile, sb_out_norm)
        xs = _ffn(xs, l, norm_ffn2, wgu2, wd2,
                  mix=(y_ssd.reshape(db * t_new, WIDTH), y_sb.reshape(db * t_new, WIDTH), wo))
        outs["ks"].append(k.reshape(db, t_new, N_HEADS, HEAD_DIM))
        outs["vs"].append(v.reshape(db, t_new, N_HEADS, HEAD_DIM))
        outs["ss"].append(_blocks_to_state(s_fin))
        outs["cs"].append(jnp.concatenate([state_conv[l], xbc], axis=1)[:, -(CONV_W - 1):, :])

    y_prompt = xp.reshape(bp, seq, d)[:, n_meta:]
    y_sample = xs.reshape(db, t_new, d)
    st = lambda name: jnp.stack(outs[name])
    to_thd = lambda a: jnp.transpose(a.reshape(depth, bp, N_HEADS, HEAD_DIM, seq), (0, 1, 4, 2, 3))
    return (y_prompt, y_sample, to_thd(kp_stack), to_thd(vp_stack), st("sp"), st("cp"),
            st("ks"), st("vs"), st("ss"), st("cs"))
```
